```python
import jax, jax.numpy as jnp
from jax import lax
import numpy as np

D_MODEL = 1024
BATCH = 2
SEQ = 8192
DEPTH = 1
DEC_BATCH = 128
DEC_SEQ = 4
PAST_LEN = 8192
PAGE_SIZE = 128

HEAD_DIM = 64
NSA_WIDTH = D_MODEL // 2
N_HEADS = NSA_WIDTH // HEAD_DIM
N_KV = 2
HPG = N_HEADS // N_KV
KV_W = N_KV * HEAD_DIM
N_BRANCH = 3
L_CMP = 32
L_SEL = 64
N_SEL = 16
WINDOW = 512
Q_BLOCK = 128
POOL_WIDTH = D_MODEL - NSA_WIDTH
POOL_WINDOWS = (2, 4, 8, 16)
POOL_GROUP = POOL_WIDTH // len(POOL_WINDOWS)
POOL_STATE = max(POOL_WINDOWS) - 1
D_FF = 2816
N_MOD = 9
IN_WIDTHS = (NSA_WIDTH, KV_W, KV_W, KV_W, KV_W, KV_W, KV_W, N_HEADS * N_BRANCH, POOL_WIDTH)
IN_DIM = sum(IN_WIDTHS)
SCALE = HEAD_DIM ** -0.5
RMS_EPS = 1e-6
NEG = -1e30
FORCE = 1e4

kernel_name = 'nsa_pool_macaron_hybrid_step'


def rmsnorm(x, g):
    xf = x.astype(jnp.float32)
    y = xf * lax.rsqrt(jnp.mean(xf * xf, axis=-1, keepdims=True) + RMS_EPS)
    return (y * g.astype(jnp.float32)).astype(x.dtype)


def swiglu(h, wi, wo):
    a, b = jnp.split(h @ wi, 2, axis=-1)
    return (jax.nn.silu(a) * b) @ wo


def in_projection(h, w_in):
    B, T, _ = h.shape
    z = h @ w_in
    parts, o = [], 0
    for w in IN_WIDTHS:
        parts.append(z[..., o:o + w])
        o += w
    q, kc, vc, ks, vs, kw, vw, g, u = parts
    kv = lambda a: a.reshape(B, T, N_KV, HEAD_DIM)
    gates = jax.nn.sigmoid(g.astype(jnp.float32)).astype(h.dtype).reshape(B, T, N_HEADS, N_BRANCH)
    return (q.reshape(B, T, N_HEADS, HEAD_DIM), kv(kc), kv(vc), kv(ks), kv(vs), kv(kw), kv(vw), gates, u)


def compress(rows, w):
    B, n, G, hd = rows.shape
    blocks = rows.reshape(B, n // L_CMP, L_CMP, G, hd)
    return jnp.einsum('bnlgh,l->bngh', blocks, w)


def cmp_attend(q, qpos, kc, vc):
    B, T = q.shape[:2]
    nc = kc.shape[1]
    qg = q.reshape(B, T, N_KV, HPG, HEAD_DIM)
    s = jnp.einsum('btgph,bngh->btgpn', qg, kc, preferred_element_type=jnp.float32) * SCALE
    valid = ((jnp.arange(nc) + 1) * L_CMP - 1)[None, :] <= qpos[:, None]
    m = valid[None, :, None, None, :]
    p = jnp.where(m, jax.nn.softmax(jnp.where(m, s, NEG), axis=-1), 0.0)
    o = jnp.einsum('btgpn,bngh->btgph', p.astype(vc.dtype), vc)
    return o.reshape(B, T, N_HEADS, HEAD_DIM), p


def select_blocks(p, qpos, n_blk):
    B, T, G, _, nc = p.shape
    r = L_SEL // L_CMP
    imp = jnp.pad(p.sum(3), ((0, 0), (0, 0), (0, 0), (0, n_blk * r - nc)))
    imp = imp.reshape(B, T, G, n_blk, r).sum(-1)
    blk = jnp.arange(n_blk)[None, :]
    cur = (qpos // L_SEL)[:, None]
    start_ok = blk * L_SEL <= qpos[:, None]
    forced = (blk == 0) | (blk == cur) | (blk == cur - 1)
    bonus = jnp.where(forced, FORCE, jnp.where(start_ok, 0.0, NEG))
    _, idx = lax.top_k(imp + bonus[None, :, None, :], min(N_SEL, n_blk))
    return idx


def sel_attend(q, qpos, idx, kg, vg):
    B, T = q.shape[:2]
    n = idx.shape[-1]
    qg = q.reshape(B, T, N_KV, HPG, HEAD_DIM)
    s = jnp.einsum('btgph,btgnlh->btgpnl', qg, kg, preferred_element_type=jnp.float32) * SCALE
    kpos = idx[..., None] * L_SEL + jnp.arange(L_SEL)
    m = (kpos <= qpos[None, :, None, None, None])[:, :, :, None]
    s = jnp.where(m, s, NEG).reshape(B, T, N_KV, HPG, n * L_SEL)
    p = jax.nn.softmax(s, axis=-1).reshape(B, T, N_KV, HPG, n, L_SEL)
    o = jnp.einsum('btgpnl,btgnlh->btgph', p.astype(vg.dtype), vg)
    return o.reshape(B, T, N_HEADS, HEAD_DIM)


def band_attend(q, qpos, k, v, kpos):
    s = jnp.einsum('bnqgph,bnkgh->bnqgpk', q, k, preferred_element_type=jnp.float32) * SCALE
    dq = qpos[:, :, None]
    dk = kpos[:, None, :]
    m = (dk <= dq) & (dk > dq - WINDOW) & (dk >= 0)
    p = jax.nn.softmax(jnp.where(m[None, :, :, None, None, :], s, NEG), axis=-1)
    return jnp.einsum('bnqgpk,bnkgh->bnqgph', p.astype(v.dtype), v)


def gather_contig(src, idx):
    B = src.shape[0]
    rows = idx[..., None] * L_SEL + jnp.arange(L_SEL)
    bi = jnp.arange(B)[:, None, None, None, None]
    gi = jnp.arange(N_KV)[None, None, :, None, None]
    return src[bi, rows, gi]


def gather_sel_paged(pool, page_table, new, idx, n_past_blk):
    DB, T = new.shape[:2]
    n_new_blk = -(-T // L_SEL)
    l = jnp.arange(L_SEL)
    bi = jnp.arange(DB)[:, None, None, None, None]
    gi = jnp.arange(N_KV)[None, None, :, None, None]
    past = idx < n_past_blk
    pos = jnp.where(past, idx, 0)[..., None] * L_SEL + l
    phys = page_table[bi, pos // PAGE_SIZE]
    from_pool = pool[phys, pos % PAGE_SIZE, gi]
    new_pad = jnp.pad(new, ((0, 0), (0, n_new_blk * L_SEL - T), (0, 0), (0, 0)))
    nrow = jnp.where(past, 0, idx - n_past_blk)[..., None] * L_SEL + l
    from_new = new_pad[bi, nrow, gi]
    return jnp.where(past[..., None, None], from_pool, from_new)


def pool_mix(u_ext, pos0, pool_w, pool_scale):
    B, L, _ = u_ext.shape
    T = L - POOL_STATE
    cs = jnp.pad(jnp.cumsum(u_ext.astype(jnp.float32), axis=1), ((0, 0), (1, 0), (0, 0)))
    t = pos0 + jnp.arange(T)
    hi = cs[:, POOL_STATE + 1:]
    u_new = u_ext[:, POOL_STATE:].astype(jnp.float32)
    diffs = []
    for gi, w in enumerate(POOL_WINDOWS):
        c0, c1 = gi * POOL_GROUP, (gi + 1) * POOL_GROUP
        lo = cs[:, POOL_STATE + 1 - w:POOL_STATE + 1 - w + T, c0:c1]
        cnt = jnp.minimum(w, t + 1).astype(jnp.float32)[None, :, None]
        diffs.append((hi[..., c0:c1] - lo) / cnt - u_new[..., c0:c1])
    d = jnp.stack(diffs, axis=2).astype(u_ext.dtype)
    y = jnp.einsum('btgc,gcd->btgd', d, pool_w).reshape(B, T, POOL_WIDTH)
    return y * pool_scale


def mixer_out(gates, o_cmp, o_sel, o_win, y_pool, w_out):
    B, T = gates.shape[:2]
    o_nsa = (gates[..., 0:1] * o_cmp + gates[..., 1:2] * o_sel + gates[..., 2:3] * o_win).reshape(B, T, NSA_WIDTH)
    return jnp.concatenate([o_nsa, y_pool], axis=-1) @ w_out


def mixer_prompt(h, w_in, w_out, cmp_w, pool_w, pool_scale):
    B, S, _ = h.shape
    q, kc_r, vc_r, ks, vs, kw, vw, gates, u = in_projection(h, w_in)
    qpos = jnp.arange(S)
    kc = compress(kc_r, cmp_w[0])
    vc = compress(vc_r, cmp_w[1])
    o_cmp, p_cmp = cmp_attend(q, qpos, kc, vc)
    idx = select_blocks(p_cmp, qpos, S // L_SEL)
    nqb = S // Q_BLOCK

    def to_blocks(a):
        return jnp.moveaxis(a.reshape(B, nqb, Q_BLOCK, *a.shape[2:]), 1, 0)

    def sel_block(args):
        qb, pb, ib = args
        return sel_attend(qb, pb, ib, gather_contig(ks, ib), gather_contig(vs, ib))

    o_sel = lax.map(sel_block, (to_blocks(q), qpos.reshape(nqb, Q_BLOCK), to_blocks(idx)))
    o_sel = jnp.moveaxis(o_sel, 0, 1).reshape(B, S, N_HEADS, HEAD_DIM)
    nwb = WINDOW // Q_BLOCK

    def band(a):
        ap = jnp.pad(a, ((0, 0), (WINDOW, 0), (0, 0), (0, 0))).reshape(B, nqb + nwb, Q_BLOCK, N_KV, HEAD_DIM)
        return jnp.concatenate([ap[:, j:j + nqb] for j in range(nwb + 1)], axis=2)

    kpos = (jnp.arange(nqb) * Q_BLOCK)[:, None] - WINDOW + jnp.arange((nwb + 1) * Q_BLOCK)[None, :]
    qb = q.reshape(B, nqb, Q_BLOCK, N_KV, HPG, HEAD_DIM)
    o_win = band_attend(qb, qpos.reshape(nqb, Q_BLOCK), band(kw), band(vw), kpos).reshape(B, S, N_HEADS, HEAD_DIM)
    u_ext = jnp.pad(u, ((0, 0), (POOL_STATE, 0), (0, 0)))
    y_pool = pool_mix(u_ext, 0, pool_w, pool_scale)
    y = mixer_out(gates, o_cmp, o_sel, o_win, y_pool, w_out)
    keep = min(WINDOW, S)
    state = (kc_r, vc_r, ks, vs, kw[:, S - keep:], vw[:, S - keep:], u_ext[:, u_ext.shape[1] - POOL_STATE:])
    return y, state


def mixer_sample(h, cache_cmp_k, cache_cmp_v, cache_sel_k, cache_sel_v, state_win_k, state_win_v,
                 state_pool, page_table, w_in, w_out, cmp_w, pool_w, pool_scale):
    DB, T, _ = h.shape
    P = page_table.shape[1] * PAGE_SIZE
    q, kc_r, vc_r, ks, vs, kw, vw, gates, u = in_projection(h, w_in)
    qpos = P + jnp.arange(T)
    past = lambda pool: pool[page_table].reshape(DB, P, N_KV, HEAD_DIM)
    n_full = (T // L_CMP) * L_CMP
    kc = jnp.concatenate([compress(past(cache_cmp_k), cmp_w[0]), compress(kc_r[:, :n_full], cmp_w[0])], axis=1)
    vc = jnp.concatenate([compress(past(cache_cmp_v), cmp_w[1]), compress(vc_r[:, :n_full], cmp_w[1])], axis=1)
    o_cmp, p_cmp = cmp_attend(q, qpos, kc, vc)
    n_past_blk = P // L_SEL
    n_blk = n_past_blk + -(-T // L_SEL)
    idx = select_blocks(p_cmp, qpos, n_blk)
    kg = gather_sel_paged(cache_sel_k, page_table, ks, idx, n_past_blk)
    vg = gather_sel_paged(cache_sel_v, page_table, vs, idx, n_past_blk)
    o_sel = sel_attend(q, qpos, idx, kg, vg)
    wb = state_win_k.shape[1]
    kwin = jnp.concatenate([state_win_k, kw], axis=1)
    vwin = jnp.concatenate([state_win_v, vw], axis=1)
    kpos = jnp.concatenate([P - wb + jnp.arange(wb), qpos])
    qb = q.reshape(DB, 1, T, N_KV, HPG, HEAD_DIM)
    o_win = band_attend(qb, qpos[None], kwin[:, None], vwin[:, None], kpos[None]).reshape(DB, T, N_HEADS, HEAD_DIM)
    u_ext = jnp.concatenate([state_pool, u], axis=1)
    y_pool = pool_mix(u_ext, P, pool_w, pool_scale)
    y = mixer_out(gates, o_cmp, o_sel, o_win, y_pool, w_out)
    keep = min(WINDOW, wb + T)
    state = (kc_r, vc_r, ks, vs, kwin[:, wb + T - keep:], vwin[:, wb + T - keep:], u_ext[:, u_ext.shape[1] - POOL_STATE:])
    return y, state


def layer(x, c, mix, w_ada, b_ada, gains, ffn1_wi, ffn1_wo, ffn2_wi, ffn2_wo):
    mod = (jax.nn.silu(c) @ w_ada + b_ada).reshape(c.shape[0], N_MOD, 1, D_MODEL)
    sh1, sc1, g1, sh2, sc2, g2, sh3, sc3, g3 = [mod[:, i] for i in range(N_MOD)]
    h = rmsnorm(x, gains[0]) * (1 + sc1) + sh1
    x = x + 0.5 * g1 * rmsnorm(swiglu(h, ffn1_wi, ffn1_wo), gains[1])
    h = rmsnorm(x, gains[2]) * (1 + sc2) + sh2
    y, state = mix(h)
    x = x + g2 * rmsnorm(y, gains[3])
    h = rmsnorm(x, gains[4]) * (1 + sc3) + sh3
    x = x + 0.5 * g3 * rmsnorm(swiglu(h, ffn2_wi, ffn2_wo), gains[5])
    return x, state


def setup_inputs(seed: int = 0) -> dict:
    key = jax.random.key(seed)
    k = jax.random.split(key, 24)
    f32 = jnp.float32
    n_pages = PAST_LEN // PAGE_SIZE
    n_used = DEC_BATCH * n_pages
    n_phys = n_used + max(1, n_used // 4)
    win_buf = min(WINDOW, PAST_LEN)

    def nrm(kk, shape, s=1.0):
        return jax.random.normal(kk, shape, f32) * s

    page_table = jax.random.permutation(k[0], n_phys)[:n_used].reshape(DEC_BATCH, n_pages).astype(jnp.int32)
    cache_shape = (DEPTH, n_phys, PAGE_SIZE, N_KV, HEAD_DIM)
    win_shape = (DEPTH, DEC_BATCH, win_buf, N_KV, HEAD_DIM)
    return {
        'x_prompt': nrm(k[1], (BATCH, SEQ, D_MODEL)),
        'x_sample': nrm(k[2], (DEC_BATCH, DEC_SEQ, D_MODEL)),
        'cache_cmp_k': nrm(k[3], cache_shape),
        'cache_cmp_v': nrm(k[4], cache_shape),
        'cache_sel_k': nrm(k[5], cache_shape),
        'cache_sel_v': nrm(k[6], cache_shape),
        'state_win_k': nrm(k[7], win_shape),
        'state_win_v': nrm(k[8], win_shape),
        'state_pool': nrm(k[9], (DEPTH, DEC_BATCH, POOL_STATE, POOL_WIDTH)),
        'page_table': page_table,
        'c_prompt': nrm(k[10], (BATCH, D_MODEL)),
        'c_sample': nrm(k[11], (DEC_BATCH, D_MODEL)),
        'w_ada': nrm(k[12], (DEPTH, D_MODEL, N_MOD * D_MODEL), 0.5 * D_MODEL ** -0.5),
        'b_ada': nrm(k[13], (DEPTH, N_MOD * D_MODEL), 0.01),
        'norm_gains': 1.0 + nrm(k[14], (DEPTH, 6, D_MODEL), 0.05),
        'ffn1_wi': nrm(k[15], (DEPTH, D_MODEL, 2 * D_FF), D_MODEL ** -0.5),
        'ffn1_wo': nrm(k[16], (DEPTH, D_FF, D_MODEL), D_FF ** -0.5),
        'ffn2_wi': nrm(k[17], (DEPTH, D_MODEL, 2 * D_FF), D_MODEL ** -0.5),
        'ffn2_wo': nrm(k[18], (DEPTH, D_FF, D_MODEL), D_FF ** -0.5),
        'w_in': nrm(k[19], (DEPTH, D_MODEL, IN_DIM), D_MODEL ** -0.5),
        'w_out': nrm(k[20], (DEPTH, NSA_WIDTH + POOL_WIDTH, D_MODEL), D_MODEL ** -0.5),
        'cmp_w': (1.0 + nrm(k[21], (DEPTH, 2, L_CMP), 0.1)) * L_CMP ** -0.5,
        'pool_w': nrm(k[22], (DEPTH, len(POOL_WINDOWS), POOL_GROUP, POOL_GROUP), POOL_GROUP ** -0.5),
        'pool_scale': 1.0 + nrm(k[23], (DEPTH, POOL_WIDTH), 0.1),
    }


def reference(x_prompt, x_sample, cache_cmp_k, cache_cmp_v, cache_sel_k, cache_sel_v, state_win_k,
              state_win_v, state_pool, page_table, c_prompt, c_sample, w_ada, b_ada, norm_gains,
              ffn1_wi, ffn1_wo, ffn2_wi, ffn2_wo, w_in, w_out, cmp_w, pool_w, pool_scale):
    xp, xs = x_prompt, x_sample
    acc_p = [[] for _ in range(7)]
    acc_s = [[] for _ in range(7)]
    for l in range(DEPTH):
        def mix_p(h):
            return mixer_prompt(h, w_in[l], w_out[l], cmp_w[l], pool_w[l], pool_scale[l])

        def mix_s(h):
            return mixer_sample(h, cache_cmp_k[l], cache_cmp_v[l], cache_sel_k[l], cache_sel_v[l],
                                state_win_k[l], state_win_v[l], state_pool[l], page_table,
                                w_in[l], w_out[l], cmp_w[l], pool_w[l], pool_scale[l])

        xp, st_p = layer(xp, c_prompt, mix_p, w_ada[l], b_ada[l], norm_gains[l],
                         ffn1_wi[l], ffn1_wo[l], ffn2_wi[l], ffn2_wo[l])
        xs, st_s = layer(xs, c_sample, mix_s, w_ada[l], b_ada[l], norm_gains[l],
                         ffn1_wi[l], ffn1_wo[l], ffn2_wi[l], ffn2_wo[l])
        for a, s in zip(acc_p, st_p):
            a.append(s)
        for a, s in zip(acc_s, st_s):
            a.append(s)
    p_cmp_k, p_cmp_v, p_sel_k, p_sel_v, p_win_k, p_win_v, p_pool = [jnp.stack(a) for a in acc_p]
    s_cmp_k, s_cmp_v, s_sel_k, s_sel_v, s_win_k, s_win_v, s_pool = [jnp.stack(a) for a in acc_s]
    return (xp, xs, p_cmp_k, p_cmp_v, p_sel_k, p_sel_v, p_win_k, p_win_v, p_pool,
            s_cmp_k, s_cmp_v, s_sel_k, s_sel_v, s_win_k, s_win_v, s_pool)
```

```python
import functools

import jax
import jax.numpy as jnp
from jax import lax
from jax.experimental import pallas as pl
from jax.experimental.pallas import tpu as pltpu

HEAD_DIM = 64
N_KV = 2
L_CMP = 32
L_SEL = 64
N_SEL = 16
WINDOW = 512
POOL_WINDOWS = (2, 4, 8, 16)
N_MOD = 9
RMS_EPS = 1e-6
NEG = -1e30
FORCE = 1e4

LANES = 128
SUBLANES = 8
VMEM_LIMIT_BYTES = 56 * 1024 * 1024

F32 = jnp.float32
BF16 = jnp.bfloat16

KV_W = N_KV * HEAD_DIM
_NT = (((1,), (1,)), ((), ()))


def _rms(x, g):
    return x * lax.rsqrt(jnp.mean(x * x, axis=-1, keepdims=True) + RMS_EPS) * g


def _silu(a):
    return a * jax.nn.sigmoid(a)


def _params(sem):
    return pltpu.CompilerParams(dimension_semantics=sem, vmem_limit_bytes=VMEM_LIMIT_BYTES)


def _resident(shape):
    return pl.BlockSpec(shape, lambda *_: (0,) * len(shape), pipeline_mode=pl.Buffered(1))


def _mod_body(c_ref, w_ref, b_ref, o_ref):
    a = _silu(c_ref[...]).astype(BF16)
    o_ref[...] = jnp.dot(a, w_ref[...].astype(BF16), preferred_element_type=F32) + b_ref[...]


def _modulation(c, w_ada, b_ada):
    n, d = c.shape
    n_pad = -(-n // SUBLANES) * SUBLANES
    c = jnp.pad(c, ((0, n_pad - n), (0, 0)))
    out = pl.pallas_call(
        _mod_body,
        out_shape=jax.ShapeDtypeStruct((n_pad, N_MOD * d), F32),
        grid=(N_MOD,),
        in_specs=[pl.BlockSpec((n_pad, d), lambda j: (0, 0)),
                  pl.BlockSpec((d, d), lambda j: (0, j)),
                  pl.BlockSpec((1, d), lambda j: (0, j))],
        out_specs=pl.BlockSpec((n_pad, d), lambda j: (0, j)),
        compiler_params=_params(("arbitrary",)),
        name="adaln_modulation",
    )(c, w_ada, b_ada.reshape(1, -1))
    return out[:n].reshape(n, N_MOD, d)


def _ffn_chunks(d_ff):
    chunks, o = [], 0
    while o < d_ff:
        w = min(512, d_ff - o)
        chunks.append((o, w))
        o += w
    return chunks


def _swiglu(h, wi_ref, wo_ref, act_ref, d_ff):
    for o, w in _ffn_chunks(d_ff):
        a = jnp.dot(h, wi_ref[:, o:o + w], preferred_element_type=F32)
        b = jnp.dot(h, wi_ref[:, d_ff + o:d_ff + o + w], preferred_element_type=F32)
        act_ref[:, o:o + w] = (_silu(a) * b).astype(BF16)
    return jnp.dot(act_ref[...], wo_ref[...], preferred_element_type=F32)


def _ffn_body(x_ref, sh_ref, sc_ref, gt_ref, gains_ref, wi_ref, wo_ref, o_ref, act_ref, *, d_ff):
    x = x_ref[...]
    h = (_rms(x, gains_ref[0:1, :]) * (1 + sc_ref[...]) + sh_ref[...]).astype(BF16)
    y = _swiglu(h, wi_ref, wo_ref, act_ref, d_ff)
    o_ref[...] = x + 0.5 * gt_ref[...] * _rms(y, gains_ref[1:2, :])


def _mod_spec(per_row, tm, d, tiles_per_seq):
    if per_row:
        return pl.BlockSpec((tm, d), lambda i: (i, 0))
    return pl.BlockSpec((None, 1, d), lambda i: (i // tiles_per_seq, 0, 0))


def _ffn(x, mods, gains, wi, wo, *, tm, per_row, tiles_per_seq):
    r, d = x.shape
    d_ff = wo.shape[0]
    ms = _mod_spec(per_row, tm, d, tiles_per_seq)
    return pl.pallas_call(
        functools.partial(_ffn_body, d_ff=d_ff),
        out_shape=jax.ShapeDtypeStruct((r, d), F32),
        grid=(r // tm,),
        in_specs=[pl.BlockSpec((tm, d), lambda i: (i, 0)), ms, ms, ms,
                  _resident((2, d)), _resident(wi.shape), _resident(wo.shape)],
        out_specs=pl.BlockSpec((tm, d), lambda i: (i, 0)),
        scratch_shapes=[pltpu.VMEM((tm, d_ff), BF16)],
        compiler_params=_params(("arbitrary",)),
        name="ffn",
    )(x, *mods, gains, wi, wo)


_QW = 8 * LANES
_O_KC, _O_VC, _O_KS, _O_VS, _O_KW, _O_VW = (_QW + i * KV_W for i in range(6))
_O_G = _QW + 6 * KV_W
_O_U = _O_G + LANES
_POOL_W = len(POOL_WINDOWS) * LANES
_IN_COLS = _O_U + _POOL_W
_HALO = 16


def _prep_w_in(w_in):
    d = w_in.shape[0]
    n_heads = (_QW // LANES)
    hpg = n_heads // N_KV
    q = w_in[:, :n_heads * HEAD_DIM].reshape(d, n_heads, HEAD_DIM)
    slot = jnp.zeros((d, n_heads, N_KV, HEAD_DIM), w_in.dtype)
    for h in range(n_heads):
        slot = slot.at[:, h, h // hpg].set(q[:, h])
    o = n_heads * HEAD_DIM
    kv = w_in[:, o:o + 6 * KV_W]
    o += 6 * KV_W
    n_g = n_heads * 3
    g = jnp.pad(w_in[:, o:o + n_g], ((0, 0), (0, LANES - n_g)))
    u = w_in[:, o + n_g:]
    return jnp.concatenate([slot.reshape(d, _QW), kv, g, u], axis=1).astype(BF16)


def _project(x_ref, sh_ref, sc_ref, gain_ref, w_ref):
    h = (_rms(x_ref[...], gain_ref[...]) * (1 + sc_ref[...]) + sh_ref[...]).astype(BF16)
    return jnp.dot(h, w_ref[...], preferred_element_type=F32)


def _pool_out(d, pw_ref, ps_ref):
    parts = [jnp.dot(d[:, g * LANES:(g + 1) * LANES].astype(BF16), pw_ref[g], preferred_element_type=F32)
             for g in range(len(POOL_WINDOWS))]
    return (jnp.concatenate(parts, axis=1) * ps_ref[...]).astype(BF16)


def _inproj_prompt_body(x_ref, sh_ref, sc_ref, gain_ref, w_ref, cw_ref, pw_ref, ps_ref,
                        kcr_ref, vcr_ref, ks_ref, vs_ref, kw_ref, vw_ref, kc_ref, vc_ref,
                        q_ref, g_ref, ksa_ref, vsa_ref, kwb_ref, vwa_ref, yp_ref, ul_ref,
                        ext_ref, *, tm):
    i = pl.program_id(1)
    z = _project(x_ref, sh_ref, sc_ref, gain_ref, w_ref)
    kcr, vcr = z[:, _O_KC:_O_KC + KV_W], z[:, _O_VC:_O_VC + KV_W]
    ks, vs = z[:, _O_KS:_O_KS + KV_W], z[:, _O_VS:_O_VS + KV_W]
    kw, vw = z[:, _O_KW:_O_KW + KV_W], z[:, _O_VW:_O_VW + KV_W]
    kcr_ref[...], vcr_ref[...] = kcr, vcr
    ks_ref[...], vs_ref[...] = ks, vs
    kw_ref[...], vw_ref[...] = kw, vw
    kc_ref[...] = (kcr * cw_ref[0]).reshape(tm // L_CMP, L_CMP, KV_W).sum(axis=1)
    vc_ref[...] = (vcr * cw_ref[1]).reshape(tm // L_CMP, L_CMP, KV_W).sum(axis=1)
    q_ref[...] = (z[:, :_QW] * (HEAD_DIM ** -0.5)).astype(BF16)
    g_ref[...] = jax.nn.sigmoid(z[:, _O_G:_O_G + LANES])
    pos = i * tm + lax.broadcasted_iota(jnp.int32, (tm, LANES), 0)
    blk = lax.broadcasted_iota(jnp.int32, (tm, LANES), 1)
    onehot = jnp.where(pos // L_SEL == blk, 1.0, 0.0).astype(BF16)
    ones = jnp.ones((tm, LANES), BF16)
    ksa_ref[...] = jnp.concatenate([ks.astype(BF16), onehot], axis=1)
    vsa_ref[...] = jnp.concatenate([vs.astype(BF16), ones], axis=1)
    kwb_ref[...] = kw.astype(BF16)
    vwa_ref[...] = jnp.concatenate([vw.astype(BF16), ones], axis=1)
    u = z[:, _O_U:_O_U + _POOL_W]

    @pl.when(i == 0)
    def _():
        ext_ref[0:_HALO, :] = jnp.zeros((_HALO, _POOL_W), F32)

    ext_ref[_HALO:, :] = u
    tpos = i * tm + lax.broadcasted_iota(jnp.int32, (tm, LANES), 0)
    diffs = []
    for g, w in enumerate(POOL_WINDOWS):
        s = ext_ref[:, g * LANES:(g + 1) * LANES]
        k = 1
        while k < w:
            s = s + pltpu.roll(s, k, 0)
            k *= 2
        cnt = jnp.minimum(w, tpos + 1).astype(F32)
        diffs.append(s[_HALO:] / cnt - u[:, g * LANES:(g + 1) * LANES])
    yp_ref[...] = _pool_out(jnp.concatenate(diffs, axis=1), pw_ref, ps_ref)
    tail = u[tm - _HALO:, :]
    ext_ref[0:_HALO, :] = tail
    ul_ref[...] = tail


def _inproj_prompt(x, sh, sc, gain, w_all, cmp_rows, pool_w, pool_scale, *, batch, seq, tm):
    d = x.shape[1]
    tps = seq // tm
    row = lambda c: pl.BlockSpec((None, tm, c), lambda b, i: (b, i, 0))
    mod = pl.BlockSpec((None, 1, d), lambda b, i: (b, 0, 0))
    f32o = lambda c: jax.ShapeDtypeStruct((batch, seq, c), F32)
    bf16o = lambda c: jax.ShapeDtypeStruct((batch, seq, c), BF16)
    out_shape = ([f32o(KV_W)] * 6
                 + [jax.ShapeDtypeStruct((batch, seq // L_CMP, KV_W), F32)] * 2
                 + [bf16o(_QW), f32o(LANES), bf16o(2 * LANES), bf16o(2 * LANES), bf16o(KV_W), bf16o(2 * LANES),
                    bf16o(_POOL_W), jax.ShapeDtypeStruct((batch, _HALO, _POOL_W), F32)])
    out_specs = ([row(KV_W)] * 6
                 + [pl.BlockSpec((None, tm // L_CMP, KV_W), lambda b, i: (b, i, 0))] * 2
                 + [row(_QW), row(LANES), row(2 * LANES), row(2 * LANES), row(KV_W), row(2 * LANES),
                    row(_POOL_W), pl.BlockSpec((None, _HALO, _POOL_W), lambda b, i: (b, 0, 0))])
    return pl.pallas_call(
        functools.partial(_inproj_prompt_body, tm=tm),
        out_shape=out_shape,
        grid=(batch, tps),
        in_specs=[pl.BlockSpec((tm, d), lambda b, i: (b * tps + i, 0)), mod, mod,
                  _resident((1, d)), _resident(w_all.shape), _resident(cmp_rows.shape),
                  _resident(pool_w.shape), _resident(pool_scale.shape)],
        out_specs=out_specs,
        scratch_shapes=[pltpu.VMEM((_HALO + tm, _POOL_W), F32)],
        compiler_params=_params(("arbitrary", "arbitrary")),
        name="inproj_prompt",
    )(x, sh, sc, gain, w_all, cmp_rows, pool_w, pool_scale)


def _inproj_sample_body(x_ref, sh_ref, sc_ref, gain_ref, w_ref, hist_ref, pw_ref, ps_ref,
                        kv_ref, q_ref, g_ref, yp_ref, u_ref, *, n_seq, n_tok, past_len):
    z = _project(x_ref, sh_ref, sc_ref, gain_ref, w_ref)
    kv_ref[...] = z[:, _O_KC:_O_KC + 6 * KV_W]
    q_ref[...] = (z[:, :_QW] * (HEAD_DIM ** -0.5)).astype(BF16)
    g_ref[...] = jax.nn.sigmoid(z[:, _O_G:_O_G + LANES])
    u = z[:, _O_U:_O_U + _POOL_W]
    u_ref[...] = u
    n_hist = hist_ref.shape[0]
    slabs = [hist_ref[j] for j in range(n_hist)] + [u[t * n_seq:(t + 1) * n_seq] for t in range(n_tok)]
    rows = []
    for t in range(n_tok):
        diffs = []
        for g, w in enumerate(POOL_WINDOWS):
            lanes = slice(g * LANES, (g + 1) * LANES)
            s = slabs[n_hist + t][:, lanes]
            for j in range(1, w):
                s = s + slabs[n_hist + t - j][:, lanes]
            diffs.append(s / float(min(w, past_len + t + 1)) - slabs[n_hist + t][:, lanes])
        rows.append(jnp.concatenate(diffs, axis=1))
    yp_ref[...] = _pool_out(jnp.concatenate(rows, axis=0), pw_ref, ps_ref)


def _inproj_sample(x, sh, sc, gain, w_all, hist, pool_w, pool_scale, *, n_seq, n_tok, past_len):
    r, d = x.shape
    full = lambda a: pl.BlockSpec(a.shape, lambda i: (0,) * a.ndim)
    out_shape = [jax.ShapeDtypeStruct((r, 6 * KV_W), F32), jax.ShapeDtypeStruct((r, _QW), BF16),
                 jax.ShapeDtypeStruct((r, LANES), F32), jax.ShapeDtypeStruct((r, _POOL_W), BF16),
                 jax.ShapeDtypeStruct((r, _POOL_W), F32)]
    args = (x, sh, sc, gain, w_all, hist, pool_w, pool_scale)
    return pl.pallas_call(
        functools.partial(_inproj_sample_body, n_seq=n_seq, n_tok=n_tok, past_len=past_len),
        out_shape=out_shape,
        grid=(1,),
        in_specs=[full(a) for a in args],
        out_specs=[pl.BlockSpec(s.shape, lambda i: (0, 0)) for s in out_shape],
        compiler_params=_params(("arbitrary",)),
        name="inproj_sample",
    )(*args)


def _select_bias(v, n_pick):
    vt = v.T
    blk = lax.broadcasted_iota(jnp.int32, vt.shape, 0).astype(F32)
    bias = jnp.full(vt.shape, NEG, F32)
    for _ in range(n_pick):
        m = jnp.max(vt, axis=0, keepdims=True)
        first = jnp.min(jnp.where(vt == m, blk, float(LANES)), axis=0, keepdims=True)
        hit = blk == first
        bias = jnp.where(hit, 0.0, bias)
        vt = jnp.where(hit, -jnp.inf, vt)
    return bias.T


def _merge_heads(x, rows):
    n_heads = x.shape[0] // rows
    hpg = n_heads // N_KV
    parts = []
    for h in range(n_heads):
        g = h // hpg
        parts.append(x[h * rows:(h + 1) * rows, g * HEAD_DIM:(g + 1) * HEAD_DIM])
    return jnp.concatenate(parts, axis=1)


def _gate_mix(gates, o_cmp, o_sel, o_win, rows):
    n_heads = o_cmp.shape[0] // rows
    hpg = n_heads // N_KV
    parts = []
    for h in range(n_heads):
        g = h // hpg
        lanes = slice(g * HEAD_DIM, (g + 1) * HEAD_DIM)
        rs = slice(h * rows, (h + 1) * rows)
        parts.append(gates[:, 3 * h:3 * h + 1] * o_cmp[rs, lanes]
                     + gates[:, 3 * h + 1:3 * h + 2] * o_sel[rs, lanes]
                     + gates[:, 3 * h + 2:3 * h + 3] * o_win[rs, lanes])
    return jnp.concatenate(parts, axis=1)


def _cmp_attend(qc, kc, vc, valid, n_heads, rows):
    nc = kc.shape[0]
    s = lax.dot_general(qc, kc, _NT, preferred_element_type=F32).reshape(n_heads, rows, nc)
    if valid is not None:
        s = jnp.where(valid[None], s, NEG)
    e = jnp.exp(s - jnp.max(s, axis=-1, keepdims=True))
    p = e / jnp.sum(e, axis=-1, keepdims=True)
    if valid is not None:
        p = jnp.where(valid[None], p, 0.0)
    o = jnp.dot(p.reshape(n_heads * rows, nc).astype(BF16), vc, preferred_element_type=F32)
    hpg = n_heads // N_KV
    imps = []
    for g in range(N_KV):
        ph = p[g * hpg]
        for h in range(1, hpg):
            ph = ph + p[g * hpg + h]
        imps.append(ph[:, :nc // 2] + ph[:, nc // 2:])
    return o, imps


def _attn_prompt_body(q_ref, g_ref, kc_ref, vc_ref, ksa_ref, vsa_ref, kwb_ref, vwa_ref, o_ref,
                      qa_ref, m_ref, acc_ref, *, tq, tk, n_heads):
    t0 = pl.program_id(1) * tq
    rows = n_heads * tq
    hpg = n_heads // N_KV
    for h in range(n_heads):
        qa_ref[h * tq:(h + 1) * tq, 0:LANES] = q_ref[:, h * LANES:(h + 1) * LANES]
    qc = qa_ref[:, 0:LANES]
    qpos = t0 + lax.broadcasted_iota(jnp.int32, (tq, 1), 0)

    nc = kc_ref.shape[0]
    col = lax.broadcasted_iota(jnp.int32, (tq, nc), 1)
    cblk = jnp.where(col < nc // 2, 2 * col, 2 * (col - nc // 2) + 1)
    valid = (cblk + 1) * L_CMP - 1 <= qpos
    o_cmp, imps = _cmp_attend(qc, kc_ref[...], vc_ref[...], valid, n_heads, tq)

    blk = lax.broadcasted_iota(jnp.int32, (tq, LANES), 1)
    cur = qpos // L_SEL
    forced = (blk == 0) | (blk == cur) | (blk == cur - 1)
    bonus = jnp.where(forced, FORCE, jnp.where(blk * L_SEL <= qpos, 0.0, NEG))
    bias = _select_bias(jnp.concatenate([imp + bonus for imp in imps], axis=0), N_SEL)
    for h in range(n_heads):
        g = h // hpg
        qa_ref[h * tq:(h + 1) * tq, LANES:2 * LANES] = bias[g * tq:(g + 1) * tq].astype(BF16)

    m_ref[...] = jnp.full(m_ref.shape, -jnp.inf, F32)
    acc_ref[...] = jnp.zeros(acc_ref.shape, F32)

    def tile(kt, causal):
        ks0 = pl.multiple_of(kt * tk, tk)
        s = lax.dot_general(qa_ref[...], ksa_ref[pl.ds(ks0, tk), :], _NT, preferred_element_type=F32)
        if causal:
            kpos = ks0 + lax.broadcasted_iota(jnp.int32, (tq, tk), 1)
            s = jnp.where((kpos <= qpos)[None], s.reshape(n_heads, tq, tk), NEG).reshape(rows, tk)
        m_prev = m_ref[...]
        m_new = jnp.maximum(m_prev, jnp.max(s, axis=-1, keepdims=True))
        alpha = jnp.exp(m_prev - m_new)
        p = jnp.exp(s - m_new[:, 0:1]).astype(BF16)
        pv = jnp.dot(p, vsa_ref[pl.ds(ks0, tk), :], preferred_element_type=F32)
        acc_ref[...] = acc_ref[...] * jnp.concatenate([alpha, alpha], axis=1) + pv
        m_ref[...] = m_new

    n_full = t0 // tk

    def full_tile(kt, carry):
        tile(kt, False)
        return carry

    lax.fori_loop(0, n_full, full_tile, 0)
    tile(n_full, True)
    acc = acc_ref[...]
    o_sel = acc[:, 0:LANES] / acc[:, LANES:2 * LANES]

    span = WINDOW + tq
    w0 = pl.multiple_of(jnp.maximum(t0 - WINDOW, 0), tq)
    s = lax.dot_general(qc, kwb_ref[pl.ds(w0, span), :], _NT, preferred_element_type=F32)
    kpos = w0 + lax.broadcasted_iota(jnp.int32, (tq, span), 1)
    band = (kpos <= qpos) & (kpos > qpos - WINDOW)
    s = jnp.where(band[None], s.reshape(n_heads, tq, span), NEG)
    p = jnp.exp(s - jnp.max(s, axis=-1, keepdims=True)).reshape(rows, span).astype(BF16)
    accw = jnp.dot(p, vwa_ref[pl.ds(w0, span), :], preferred_element_type=F32)
    o_win = accw[:, 0:LANES] / accw[:, LANES:2 * LANES]

    o_ref[...] = _gate_mix(g_ref[...], o_cmp, o_sel, o_win, tq).astype(BF16)


def _attn_prompt(q, gates, kc, vc, ksa, vsa, kwb, vwa, *, tq, tk):
    batch, seq, qw = q.shape
    n_heads = qw // LANES
    per_b = lambda a: pl.BlockSpec((None,) + a.shape[1:], lambda b, i: (b,) + (0,) * (a.ndim - 1))
    row = lambda c: pl.BlockSpec((None, tq, c), lambda b, i: (b, i, 0))
    return pl.pallas_call(
        functools.partial(_attn_prompt_body, tq=tq, tk=tk, n_heads=n_heads),
        out_shape=jax.ShapeDtypeStruct((batch, seq, n_heads * HEAD_DIM), BF16),
        grid=(batch, seq // tq),
        in_specs=[row(qw), row(LANES), per_b(kc), per_b(vc), per_b(ksa), per_b(vsa), per_b(kwb), per_b(vwa)],
        out_specs=row(n_heads * HEAD_DIM),
        scratch_shapes=[pltpu.VMEM((n_heads * tq, 2 * LANES), BF16),
                        pltpu.VMEM((n_heads * tq, LANES), F32),
                        pltpu.VMEM((n_heads * tq, 2 * LANES), F32)],
        compiler_params=_params(("arbitrary", "arbitrary")),
        name="attn_prompt",
    )(q, gates, kc, vc, ksa, vsa, kwb, vwa)


_TP = SUBLANES
_PAGES_PER_STEP = 2


def _attn_sample_body(pt_ref, q_ref, g_ref, new_ref, swk_ref, swv_ref, cw_ref,
                      cck_hbm, ccv_hbm, csk_hbm, csv_hbm, o_ref,
                      bufs, sems, kcn_ref, vcn_ref, ka_ref, va_ref, kn_ref, vn_ref, kw_ref, vw_ref, qa_ref,
                      *, n_heads, n_tok, n_pages, page):
    i = pl.program_id(0)
    n_seq = pl.num_programs(0)
    slot = i % 2
    caches = (cck_hbm, ccv_hbm, csk_hbm, csv_hbm)
    hpg = n_heads // N_KV
    rows = n_heads * _TP
    past = n_pages * page

    def page_copy(a, seq, slot_, p):
        return pltpu.make_async_copy(caches[a].at[pt_ref[seq, p]], bufs.at[a, slot_, p], sems.at[a, slot_])

    def start_all(seq, slot_):
        def body(p, c):
            for a in range(4):
                page_copy(a, seq, slot_, p).start()
            return c
        lax.fori_loop(0, n_pages, body, 0)

    @pl.when(i == 0)
    def _():
        start_all(0, 0)
        kpos = lax.broadcasted_iota(jnp.int32, (past, LANES), 0)
        blk = lax.broadcasted_iota(jnp.int32, (past, LANES), 1)
        ka_ref[:, LANES:] = jnp.where(kpos // L_SEL == blk, 1.0, 0.0).astype(BF16)
        va_ref[:, LANES:] = jnp.ones((past, LANES), BF16)
        kn_ref[...] = jnp.zeros(kn_ref.shape, BF16)
        vn_ref[...] = jnp.zeros(vn_ref.shape, BF16)
        kw_ref[...] = jnp.zeros(kw_ref.shape, BF16)
        vw_ref[...] = jnp.zeros(vw_ref.shape, BF16)
        qa_ref[...] = jnp.zeros(qa_ref.shape, BF16)

    @pl.when(i + 1 < n_seq)
    def _():
        start_all(i + 1, 1 - slot)

    for h in range(n_heads):
        qa_ref[h * _TP:h * _TP + n_tok, 0:LANES] = q_ref[:, h * LANES:(h + 1) * LANES]
    qc = qa_ref[:, 0:LANES]
    tok = lax.broadcasted_iota(jnp.int32, (_TP, 1), 0)

    def wait_all(a):
        def body(p, c):
            page_copy(a, 0, slot, p).wait()
            return c
        lax.fori_loop(0, n_pages, body, 0)

    wait_all(0)
    wait_all(1)
    per = page // L_CMP

    def compress(pp, c):
        r0 = pl.multiple_of(pp * (_PAGES_PER_STEP * per), _PAGES_PER_STEP * per)
        for a, dst in ((0, kcn_ref), (1, vcn_ref)):
            parts = []
            for j in range(_PAGES_PER_STEP):
                pg = bufs[a, slot, pp * _PAGES_PER_STEP + j]
                parts.append((pg * cw_ref[a]).reshape(per, L_CMP, KV_W).sum(axis=1))
            dst[pl.ds(r0, _PAGES_PER_STEP * per), :] = jnp.concatenate(parts, axis=0)
        return c

    lax.fori_loop(0, n_pages // _PAGES_PER_STEP, compress, 0)
    nc = n_pages * per
    perm = lambda ref: jnp.concatenate([ref[pl.ds(0, nc // 2, stride=2), :],
                                        ref[pl.ds(1, nc // 2, stride=2), :]], axis=0).astype(BF16)
    o_cmp, imps = _cmp_attend(qc, perm(kcn_ref), perm(vcn_ref), None, n_heads, _TP)

    n_blk = past // L_SEL
    blk = lax.broadcasted_iota(jnp.int32, (_TP, LANES), 1)
    bonus = jnp.where((blk == 0) | (blk == n_blk - 1), FORCE, 0.0)
    v = jnp.concatenate([imp + bonus for imp in imps]
                        + [jnp.full((LANES - N_KV * _TP, LANES), NEG, F32)], axis=0)
    bias = _select_bias(v, N_SEL - 1)
    for h in range(n_heads):
        g = h // hpg
        qa_ref[h * _TP:(h + 1) * _TP, LANES:] = bias[g * _TP:(g + 1) * _TP].astype(BF16)

    wait_all(2)
    wait_all(3)

    def cast(p, c):
        r0 = pl.multiple_of(p * page, page)
        ka_ref[pl.ds(r0, page), 0:LANES] = bufs[2, slot, p].astype(BF16)
        va_ref[pl.ds(r0, page), 0:LANES] = bufs[3, slot, p].astype(BF16)
        return c

    lax.fori_loop(0, n_pages, cast, 0)
    new = new_ref[...]
    kn_ref[0:n_tok, :] = new[:, 2 * KV_W:3 * KV_W].astype(BF16)
    vn_ref[0:n_tok, 0:LANES] = new[:, 3 * KV_W:4 * KV_W].astype(BF16)
    vn_ref[0:n_tok, LANES:] = jnp.ones((n_tok, LANES), BF16)
    s_past = lax.dot_general(qa_ref[...], ka_ref[...], _NT, preferred_element_type=F32)
    s_new = lax.dot_general(qc, kn_ref[...], _NT, preferred_element_type=F32)
    jn = lax.broadcasted_iota(jnp.int32, (_TP, LANES), 1)
    s_new = jnp.where(((jn <= tok) & (jn < n_tok))[None], s_new.reshape(n_heads, _TP, LANES), NEG)
    s_new = s_new.reshape(rows, LANES)
    m = jnp.maximum(jnp.max(s_past, axis=-1, keepdims=True), jnp.max(s_new, axis=-1, keepdims=True))
    acc = (jnp.dot(jnp.exp(s_past - m).astype(BF16), va_ref[...], preferred_element_type=F32)
           + jnp.dot(jnp.exp(s_new - m).astype(BF16), vn_ref[...], preferred_element_type=F32))
    o_sel = acc[:, 0:LANES] / acc[:, LANES:]

    wb = swk_ref.shape[0]
    kw_ref[0:wb, :] = swk_ref[...].astype(BF16)
    vw_ref[0:wb, 0:LANES] = swv_ref[...].astype(BF16)
    vw_ref[0:wb, LANES:] = jnp.ones((wb, LANES), BF16)
    kw_ref[wb:wb + n_tok, :] = new[:, 4 * KV_W:5 * KV_W].astype(BF16)
    vw_ref[wb:wb + n_tok, 0:LANES] = new[:, 5 * KV_W:6 * KV_W].astype(BF16)
    vw_ref[wb:wb + n_tok, LANES:] = jnp.ones((n_tok, LANES), BF16)
    span = kw_ref.shape[0]
    s = lax.dot_general(qc, kw_ref[...], _NT, preferred_element_type=F32)
    c = lax.broadcasted_iota(jnp.int32, (_TP, span), 1)
    kpos = jnp.where(c < wb, c - wb, c - wb)
    band = (kpos <= tok) & (kpos > tok - WINDOW) & (c < wb + n_tok)
    s = jnp.where(band[None], s.reshape(n_heads, _TP, span), NEG)
    p = jnp.exp(s - jnp.max(s, axis=-1, keepdims=True)).reshape(rows, span).astype(BF16)
    accw = jnp.dot(p, vw_ref[...], preferred_element_type=F32)
    o_win = accw[:, 0:LANES] / accw[:, LANES:]

    gates = jnp.concatenate([g_ref[...], jnp.zeros((_TP - n_tok, LANES), F32)], axis=0)
    o_ref[...] = _gate_mix(gates, o_cmp, o_sel, o_win, _TP)[0:n_tok].astype(BF16)


def _attn_sample(page_table, q, gates, new, swk, swv, cmp_rows, cck, ccv, csk, csv):
    n_seq, n_tok, qw = q.shape
    n_heads = qw // LANES
    n_pages = page_table.shape[1]
    page = cck.shape[1]
    past = n_pages * page
    wb = swk.shape[1]
    span = -(-(wb + n_tok) // LANES) * LANES
    per_seq = lambda a: pl.BlockSpec((None,) + a.shape[1:], lambda i, pt: (i,) + (0,) * (a.ndim - 1))
    hbm = pl.BlockSpec(memory_space=pl.ANY)
    grid_spec = pltpu.PrefetchScalarGridSpec(
        num_scalar_prefetch=1,
        grid=(n_seq,),
        in_specs=[per_seq(q), per_seq(gates), per_seq(new), per_seq(swk), per_seq(swv),
                  pl.BlockSpec(cmp_rows.shape, lambda i, pt: (0, 0, 0)), hbm, hbm, hbm, hbm],
        out_specs=pl.BlockSpec((None, n_tok, n_heads * HEAD_DIM), lambda i, pt: (i, 0, 0)),
        scratch_shapes=[pltpu.VMEM((4, 2, n_pages, page, KV_W), F32),
                        pltpu.SemaphoreType.DMA((4, 2)),
                        pltpu.VMEM((past // L_CMP, KV_W), F32),
                        pltpu.VMEM((past // L_CMP, KV_W), F32),
                        pltpu.VMEM((past, 2 * LANES), BF16),
                        pltpu.VMEM((past, 2 * LANES), BF16),
                        pltpu.VMEM((LANES, KV_W), BF16),
                        pltpu.VMEM((LANES, 2 * LANES), BF16),
                        pltpu.VMEM((span, KV_W), BF16),
                        pltpu.VMEM((span, 2 * LANES), BF16),
                        pltpu.VMEM((n_heads * _TP, 2 * LANES), BF16)])
    return pl.pallas_call(
        functools.partial(_attn_sample_body, n_heads=n_heads, n_tok=n_tok, n_pages=n_pages, page=page),
        out_shape=jax.ShapeDtypeStruct((n_seq, n_tok, n_heads * HEAD_DIM), BF16),
        grid_spec=grid_spec,
        compiler_params=_params(("arbitrary",)),
        name="attn_sample",
    )(page_table, q, gates, new, swk, swv, cmp_rows, cck, ccv, csk, csv)


def _out_ffn_body(on_ref, yp_ref, x_ref, g2_ref, sh_ref, sc_ref, gt_ref, gains_ref, wout_ref, wi_ref, wo_ref,
                  o_ref, act_ref, *, d_ff):
    half = on_ref.shape[1]
    y = (jnp.dot(on_ref[...], wout_ref[0:half, :], preferred_element_type=F32)
         + jnp.dot(yp_ref[...], wout_ref[half:, :], preferred_element_type=F32))
    x = x_ref[...] + g2_ref[...] * _rms(y, gains_ref[0:1, :])
    h = (_rms(x, gains_ref[1:2, :]) * (1 + sc_ref[...]) + sh_ref[...]).astype(BF16)
    y = _swiglu(h, wi_ref, wo_ref, act_ref, d_ff)
    o_ref[...] = x + 0.5 * gt_ref[...] * _rms(y, gains_ref[2:3, :])


def _out_ffn(o_nsa, y_pool, x, mods, gains, w_out, wi, wo, *, tm, per_row, tiles_per_seq):
    r, d = x.shape
    d_ff = wo.shape[0]
    ms = _mod_spec(per_row, tm, d, tiles_per_seq)
    row = lambda c: pl.BlockSpec((tm, c), lambda i: (i, 0))
    return pl.pallas_call(
        functools.partial(_out_ffn_body, d_ff=d_ff),
        out_shape=jax.ShapeDtypeStruct((r, d), F32),
        grid=(r // tm,),
        in_specs=[row(o_nsa.shape[1]), row(y_pool.shape[1]), row(d), ms, ms, ms, ms,
                  _resident((3, d)), _resident(w_out.shape), _resident(wi.shape), _resident(wo.shape)],
        out_specs=row(d),
        scratch_shapes=[pltpu.VMEM((tm, d_ff), BF16)],
        compiler_params=_params(("arbitrary",)),
        name="out_ffn",
    )(o_nsa, y_pool, x, *mods, gains, w_out, wi, wo)


def _even_odd(a):
    return jnp.concatenate([a[:, 0::2], a[:, 1::2]], axis=1).astype(BF16)


def kernel(x_prompt, x_sample, cache_cmp_k, cache_cmp_v, cache_sel_k, cache_sel_v, state_win_k, state_win_v,
           state_pool, page_table, c_prompt, c_sample, w_ada, b_ada, norm_gains, ffn1_wi, ffn1_wo, ffn2_wi,
           ffn2_wo, w_in, w_out, cmp_w, pool_w, pool_scale):
    depth = w_ada.shape[0]
    batch, seq, d = x_prompt.shape
    n_seq, n_tok, _ = x_sample.shape
    n_pages = page_table.shape[1]
    page = cache_cmp_k.shape[2]
    past = n_pages * page
    tm, tq, tk = 512, 128, 512
    assert depth == 1 and seq % tm == 0 and seq // L_SEL == LANES and past // L_SEL == LANES
    assert n_tok <= _TP and past % L_SEL == 0 and past >= max(POOL_WINDOWS) and seq >= WINDOW + tq
    assert state_pool.shape[2] == max(POOL_WINDOWS) - 1 and cache_cmp_k.shape[3:] == (N_KV, HEAD_DIM)
    l = 0
    n_rows_s = n_seq * n_tok
    tps = seq // tm

    mod = _modulation(jnp.concatenate([c_prompt, c_sample], axis=0), w_ada[l], b_ada[l])
    mod_p = [mod[:batch, k][:, None, :] for k in range(N_MOD)]
    mod_s = [jnp.tile(mod[batch:, k], (n_tok, 1)) for k in range(N_MOD)]
    gains = norm_gains[l]
    wi1, wo1 = ffn1_wi[l].astype(BF16), ffn1_wo[l].astype(BF16)
    wi2, wo2 = ffn2_wi[l].astype(BF16), ffn2_wo[l].astype(BF16)
    w_all = _prep_w_in(w_in[l])
    w_o = w_out[l].astype(BF16)
    pw = pool_w[l].astype(BF16)
    ps = pool_scale[l].reshape(1, -1)
    cmp_rows_p = jnp.tile(cmp_w[l][:, :, None], (1, tm // L_CMP, KV_W))
    cmp_rows_s = jnp.tile(cmp_w[l][:, :, None], (1, page // L_CMP, KV_W))

    xp = x_prompt.reshape(batch * seq, d)
    xs = jnp.swapaxes(x_sample, 0, 1).reshape(n_rows_s, d)

    xp = _ffn(xp, mod_p[0:3], gains[0:2], wi1, wo1, tm=tm, per_row=False, tiles_per_seq=tps)
    xs = _ffn(xs, mod_s[0:3], gains[0:2], wi1, wo1, tm=n_rows_s, per_row=True, tiles_per_seq=1)

    (kcr, vcr, ks, vs, kw, vw, kc, vc, qp, gp, ksa, vsa, kwb, vwa, yp_p, u_tail) = _inproj_prompt(
        xp, mod_p[3], mod_p[4], gains[2:3], w_all, cmp_rows_p, pw, ps, batch=batch, seq=seq, tm=tm)
    on_p = _attn_prompt(qp, gp, _even_odd(kc), _even_odd(vc), ksa, vsa, kwb, vwa, tq=tq, tk=tk)

    hist = jnp.swapaxes(state_pool[l], 0, 1)
    kv_s, q_s, g_s, yp_s, u_s = _inproj_sample(
        xs, mod_s[3], mod_s[4], gains[2:3], w_all, hist, pw, ps, n_seq=n_seq, n_tok=n_tok, past_len=past)
    by_seq = lambda a: jnp.swapaxes(a.reshape(n_tok, n_seq, a.shape[-1]), 0, 1)
    new_s = by_seq(kv_s)
    pages = lambda c: c[l].reshape(c.shape[1], page, KV_W)
    swk = state_win_k[l].reshape(n_seq, -1, KV_W)
    swv = state_win_v[l].reshape(n_seq, -1, KV_W)
    on_s = _attn_sample(page_table, by_seq(q_s), by_seq(g_s), new_s, swk, swv, cmp_rows_s,
                        pages(cache_cmp_k), pages(cache_cmp_v), pages(cache_sel_k), pages(cache_sel_v))
    on_s = jnp.swapaxes(on_s, 0, 1).reshape(n_rows_s, -1)

    xp = _out_ffn(on_p.reshape(batch * seq, -1), yp_p.reshape(batch * seq, -1), xp,
                  [mod_p[5], mod_p[6], mod_p[7], mod_p[8]], gains[3:6], w_o, wi2, wo2,
                  tm=tm, per_row=False, tiles_per_seq=tps)
    xs = _out_ffn(on_s, yp_s, xs, [mod_s[5], mod_s[6], mod_s[7], mod_s[8]], gains[3:6], w_o, wi2, wo2,
                  tm=n_rows_s, per_row=True, tiles_per_seq=1)

    heads = lambda a: a.reshape(a.shape[:-1] + (N_KV, HEAD_DIM))[None]
    keep = min(WINDOW, seq)
    n_state = state_pool.shape[2]
    y_prompt = xp.reshape(batch, seq, d)
    y_sample = jnp.swapaxes(xs.reshape(n_tok, n_seq, d), 0, 1)
    p_out = (heads(kcr), heads(vcr), heads(ks), heads(vs), heads(kw[:, seq - keep:]), heads(vw[:, seq - keep:]),
             u_tail[:, _HALO - n_state:][None])
    new4 = [heads(new_s[..., j * KV_W:(j + 1) * KV_W]) for j in range(6)]
    keep_s = min(WINDOW, state_win_k.shape[2] + n_tok)
    win_k = jnp.concatenate([state_win_k, new4[4]], axis=2)
    win_v = jnp.concatenate([state_win_v, new4[5]], axis=2)
    pool_s = jnp.concatenate([state_pool[l], by_seq(u_s)], axis=1)
    s_out = (new4[0], new4[1], new4[2], new4[3], win_k[:, :, win_k.shape[2] - keep_s:],
             win_v[:, :, win_v.shape[2] - keep_s:], pool_s[:, pool_s.shape[1] - n_state:][None])
    return (y_prompt, y_sample) + p_out + s_out
```

```python
import functools

import jax
import jax.numpy as jnp
from jax import lax
from jax.experimental import pallas as pl
from jax.experimental.pallas import tpu as pltpu

HEAD_DIM = 64
N_KV = 2
L_CMP = 32
L_SEL = 64
N_SEL = 16
WINDOW = 512
POOL_WINDOWS = (2, 4, 8, 16)
N_MOD = 9
RMS_EPS = 1e-6
NEG = -1e30
FORCE = 1e4

LANES = 128
SUBLANES = 8
VMEM_LIMIT_BYTES = 56 * 1024 * 1024

F32 = jnp.float32
BF16 = jnp.bfloat16

KV_W = N_KV * HEAD_DIM
_NT = (((1,), (1,)), ((), ()))


def _rms(x, g):
    return x * lax.rsqrt(jnp.mean(x * x, axis=-1, keepdims=True) + RMS_EPS) * g


def _silu(a):
    return a * jax.nn.sigmoid(a)


def _params(sem):
    return pltpu.CompilerParams(dimension_semantics=sem, vmem_limit_bytes=VMEM_LIMIT_BYTES)


def _resident(shape):
    return pl.BlockSpec(shape, lambda *_: (0,) * len(shape), pipeline_mode=pl.Buffered(1))


def _mod_body(c_ref, w_ref, b_ref, o_ref):
    a = _silu(c_ref[...]).astype(BF16)
    o_ref[...] = jnp.dot(a, w_ref[...].astype(BF16), preferred_element_type=F32) + b_ref[...]


def _modulation(c, w_ada, b_ada):
    n, d = c.shape
    n_pad = -(-n // SUBLANES) * SUBLANES
    c = jnp.pad(c, ((0, n_pad - n), (0, 0)))
    out = pl.pallas_call(
        _mod_body,
        out_shape=jax.ShapeDtypeStruct((n_pad, N_MOD * d), F32),
        grid=(N_MOD,),
        in_specs=[pl.BlockSpec((n_pad, d), lambda j: (0, 0)),
                  pl.BlockSpec((d, d), lambda j: (0, j)),
                  pl.BlockSpec((1, d), lambda j: (0, j))],
        out_specs=pl.BlockSpec((n_pad, d), lambda j: (0, j)),
        compiler_params=_params(("arbitrary",)),
        name="adaln_modulation",
    )(c, w_ada, b_ada.reshape(1, -1))
    return out[:n].reshape(n, N_MOD, d)


def _ffn_chunks(d_ff):
    chunks, o = [], 0
    while o < d_ff:
        w = min(512, d_ff - o)
        chunks.append((o, w))
        o += w
    return chunks


def _swiglu(h, wi_ref, wo_ref, act_ref, d_ff):
    for o, w in _ffn_chunks(d_ff):
        a = jnp.dot(h, wi_ref[:, o:o + w], preferred_element_type=F32)
        b = jnp.dot(h, wi_ref[:, d_ff + o:d_ff + o + w], preferred_element_type=F32)
        act_ref[:, o:o + w] = (_silu(a) * b).astype(BF16)
    return jnp.dot(act_ref[...], wo_ref[...], preferred_element_type=F32)


def _ffn_body(x_ref, sh_ref, sc_ref, gt_ref, gains_ref, wi_ref, wo_ref, o_ref, act_ref, *, d_ff):
    x = x_ref[...]
    h = (_rms(x, gains_ref[0:1, :]) * (1 + sc_ref[...]) + sh_ref[...]).astype(BF16)
    y = _swiglu(h, wi_ref, wo_ref, act_ref, d_ff)
    o_ref[...] = x + 0.5 * gt_ref[...] * _rms(y, gains_ref[1:2, :])


def _mod_spec(per_row, tm, d, tiles_per_seq):
    if per_row:
        return pl.BlockSpec((tm, d), lambda i: (i, 0))
    return pl.BlockSpec((None, 1, d), lambda i: (i // tiles_per_seq, 0, 0))


def _ffn(x, mods, gains, wi, wo, *, tm, per_row, tiles_per_seq):
    r, d = x.shape
    d_ff = wo.shape[0]
    ms = _mod_spec(per_row, tm, d, tiles_per_seq)
    return pl.pallas_call(
        functools.partial(_ffn_body, d_ff=d_ff),
        out_shape=jax.ShapeDtypeStruct((r, d), F32),
        grid=(r // tm,),
        in_specs=[pl.BlockSpec((tm, d), lambda i: (i, 0)), ms, ms, ms,
                  _resident((2, d)), _resident(wi.shape), _resident(wo.shape)],
        out_specs=pl.BlockSpec((tm, d), lambda i: (i, 0)),
        scratch_shapes=[pltpu.VMEM((tm, d_ff), BF16)],
        compiler_params=_params(("arbitrary",)),
        name="ffn",
    )(x, *mods, gains, wi, wo)


_QW = 8 * LANES
_O_KC, _O_VC, _O_KS, _O_VS, _O_KW, _O_VW = (_QW + i * KV_W for i in range(6))
_O_G = _QW + 6 * KV_W
_O_U = _O_G + LANES
_POOL_W = len(POOL_WINDOWS) * LANES
_IN_COLS = _O_U + _POOL_W
_HALO = 16


def _prep_w_in(w_in):
    d = w_in.shape[0]
    n_heads = (_QW // LANES)
    hpg = n_heads // N_KV
    q = w_in[:, :n_heads * HEAD_DIM].reshape(d, n_heads, HEAD_DIM)
    slot = jnp.zeros((d, n_heads, N_KV, HEAD_DIM), w_in.dtype)
    for h in range(n_heads):
        slot = slot.at[:, h, h // hpg].set(q[:, h])
    o = n_heads * HEAD_DIM
    kv = w_in[:, o:o + 6 * KV_W]
    o += 6 * KV_W
    n_g = n_heads * 3
    g = jnp.pad(w_in[:, o:o + n_g], ((0, 0), (0, LANES - n_g)))
    u = w_in[:, o + n_g:]
    return jnp.concatenate([slot.reshape(d, _QW), kv, g, u], axis=1).astype(BF16)


def _project(x_ref, sh_ref, sc_ref, gain_ref, w_ref):
    h = (_rms(x_ref[...], gain_ref[...]) * (1 + sc_ref[...]) + sh_ref[...]).astype(BF16)
    return jnp.dot(h, w_ref[...], preferred_element_type=F32)


def _pool_out(d, pw_ref, ps_ref):
    parts = [jnp.dot(d[:, g * LANES:(g + 1) * LANES].astype(BF16), pw_ref[g], preferred_element_type=F32)
             for g in range(len(POOL_WINDOWS))]
    return (jnp.concatenate(parts, axis=1) * ps_ref[...]).astype(BF16)


def _inproj_prompt_body(x_ref, sh_ref, sc_ref, gain_ref, w_ref, cw_ref, pw_ref, ps_ref,
                        kcr_ref, vcr_ref, ks_ref, vs_ref, kw_ref, vw_ref, kc_ref, vc_ref,
                        q_ref, g_ref, ksa_ref, vsa_ref, kwb_ref, vwa_ref, yp_ref, ul_ref,
                        ext_ref, *, tm):
    i = pl.program_id(1)
    z = _project(x_ref, sh_ref, sc_ref, gain_ref, w_ref)
    kcr, vcr = z[:, _O_KC:_O_KC + KV_W], z[:, _O_VC:_O_VC + KV_W]
    ks, vs = z[:, _O_KS:_O_KS + KV_W], z[:, _O_VS:_O_VS + KV_W]
    kw, vw = z[:, _O_KW:_O_KW + KV_W], z[:, _O_VW:_O_VW + KV_W]
    kcr_ref[...], vcr_ref[...] = kcr, vcr
    ks_ref[...], vs_ref[...] = ks, vs
    kw_ref[...], vw_ref[...] = kw, vw
    kc_ref[...] = (kcr * cw_ref[0]).reshape(tm // L_CMP, L_CMP, KV_W).sum(axis=1)
    vc_ref[...] = (vcr * cw_ref[1]).reshape(tm // L_CMP, L_CMP, KV_W).sum(axis=1)
    q_ref[...] = (z[:, :_QW] * (HEAD_DIM ** -0.5)).astype(BF16)
    g_ref[...] = jax.nn.sigmoid(z[:, _O_G:_O_G + LANES])
    pos = i * tm + lax.broadcasted_iota(jnp.int32, (tm, LANES), 0)
    blk = lax.broadcasted_iota(jnp.int32, (tm, LANES), 1)
    onehot = jnp.where(pos // L_SEL == blk, 1.0, 0.0).astype(BF16)
    ones = jnp.ones((tm, LANES), BF16)
    ksa_ref[...] = jnp.concatenate([ks.astype(BF16), onehot], axis=1)
    vsa_ref[...] = jnp.concatenate([vs.astype(BF16), ones], axis=1)
    kwb_ref[...] = kw.astype(BF16)
    vwa_ref[...] = jnp.concatenate([vw.astype(BF16), ones], axis=1)
    u = z[:, _O_U:_O_U + _POOL_W]

    @pl.when(i == 0)
    def _():
        ext_ref[0:_HALO, :] = jnp.zeros((_HALO, _POOL_W), F32)

    ext_ref[_HALO:, :] = u
    tpos = i * tm + lax.broadcasted_iota(jnp.int32, (tm, LANES), 0)
    diffs = []
    for g, w in enumerate(POOL_WINDOWS):
        s = ext_ref[:, g * LANES:(g + 1) * LANES]
        k = 1
        while k < w:
            s = s + pltpu.roll(s, k, 0)
            k *= 2
        cnt = jnp.minimum(w, tpos + 1).astype(F32)
        diffs.append(s[_HALO:] / cnt - u[:, g * LANES:(g + 1) * LANES])
    yp_ref[...] = _pool_out(jnp.concatenate(diffs, axis=1), pw_ref, ps_ref)
    tail = u[tm - _HALO:, :]
    ext_ref[0:_HALO, :] = tail
    ul_ref[...] = tail


def _inproj_prompt(x, sh, sc, gain, w_all, cmp_rows, pool_w, pool_scale, *, batch, seq, tm):
    d = x.shape[1]
    tps = seq // tm
    row = lambda c: pl.BlockSpec((None, tm, c), lambda b, i: (b, i, 0))
    mod = pl.BlockSpec((None, 1, d), lambda b, i: (b, 0, 0))
    f32o = lambda c: jax.ShapeDtypeStruct((batch, seq, c), F32)
    bf16o = lambda c: jax.ShapeDtypeStruct((batch, seq, c), BF16)
    out_shape = ([f32o(KV_W)] * 6
                 + [jax.ShapeDtypeStruct((batch, seq // L_CMP, KV_W), F32)] * 2
                 + [bf16o(_QW), f32o(LANES), bf16o(2 * LANES), bf16o(2 * LANES), bf16o(KV_W), bf16o(2 * LANES),
                    bf16o(_POOL_W), jax.ShapeDtypeStruct((batch, _HALO, _POOL_W), F32)])
    out_specs = ([row(KV_W)] * 6
                 + [pl.BlockSpec((None, tm // L_CMP, KV_W), lambda b, i: (b, i, 0))] * 2
                 + [row(_QW), row(LANES), row(2 * LANES), row(2 * LANES), row(KV_W), row(2 * LANES),
                    row(_POOL_W), pl.BlockSpec((None, _HALO, _POOL_W), lambda b, i: (b, 0, 0))])
    return pl.pallas_call(
        functools.partial(_inproj_prompt_body, tm=tm),
        out_shape=out_shape,
        grid=(batch, tps),
        in_specs=[pl.BlockSpec((tm, d), lambda b, i: (b * tps + i, 0)), mod, mod,
                  _resident((1, d)), _resident(w_all.shape), _resident(cmp_rows.shape),
                  _resident(pool_w.shape), _resident(pool_scale.shape)],
        out_specs=out_specs,
        scratch_shapes=[pltpu.VMEM((_HALO + tm, _POOL_W), F32)],
        compiler_params=_params(("arbitrary", "arbitrary")),
        name="inproj_prompt",
    )(x, sh, sc, gain, w_all, cmp_rows, pool_w, pool_scale)


def _inproj_sample_body(x_ref, sh_ref, sc_ref, gain_ref, w_ref, hist_ref, pw_ref, ps_ref,
                        kv_ref, q_ref, g_ref, yp_ref, u_ref, *, n_seq, n_tok, past_len):
    z = _project(x_ref, sh_ref, sc_ref, gain_ref, w_ref)
    kv_ref[...] = z[:, _O_KC:_O_KC + 6 * KV_W]
    q_ref[...] = (z[:, :_QW] * (HEAD_DIM ** -0.5)).astype(BF16)
    g_ref[...] = jax.nn.sigmoid(z[:, _O_G:_O_G + LANES])
    u = z[:, _O_U:_O_U + _POOL_W]
    u_ref[...] = u
    n_hist = hist_ref.shape[0]
    slabs = [hist_ref[j] for j in range(n_hist)] + [u[t * n_seq:(t + 1) * n_seq] for t in range(n_tok)]
    rows = []
    for t in range(n_tok):
        diffs = []
        for g, w in enumerate(POOL_WINDOWS):
            lanes = slice(g * LANES, (g + 1) * LANES)
            s = slabs[n_hist + t][:, lanes]
            for j in range(1, w):
                s = s + slabs[n_hist + t - j][:, lanes]
            diffs.append(s / float(min(w, past_len + t + 1)) - slabs[n_hist + t][:, lanes])
        rows.append(jnp.concatenate(diffs, axis=1))
    yp_ref[...] = _pool_out(jnp.concatenate(rows, axis=0), pw_ref, ps_ref)


def _inproj_sample(x, sh, sc, gain, w_all, hist, pool_w, pool_scale, *, n_seq, n_tok, past_len):
    r, d = x.shape
    full = lambda a: pl.BlockSpec(a.shape, lambda i: (0,) * a.ndim)
    out_shape = [jax.ShapeDtypeStruct((r, 6 * KV_W), F32), jax.ShapeDtypeStruct((r, _QW), BF16),
                 jax.ShapeDtypeStruct((r, LANES), F32), jax.ShapeDtypeStruct((r, _POOL_W), BF16),
                 jax.ShapeDtypeStruct((r, _POOL_W), F32)]
    args = (x, sh, sc, gain, w_all, hist, pool_w, pool_scale)
    return pl.pallas_call(
        functools.partial(_inproj_sample_body, n_seq=n_seq, n_tok=n_tok, past_len=past_len),
        out_shape=out_shape,
        grid=(1,),
        in_specs=[full(a) for a in args],
        out_specs=[pl.BlockSpec(s.shape, lambda i: (0, 0)) for s in out_shape],
        compiler_params=_params(("arbitrary",)),
        name="inproj_sample",
    )(*args)


def _select_bias(v, n_pick):
    vt = v.T
    blk = lax.broadcasted_iota(jnp.int32, vt.shape, 0).astype(F32)
    bias = jnp.full(vt.shape, NEG, F32)
    for _ in range(n_pick):
        m = jnp.max(vt, axis=0, keepdims=True)
        first = jnp.min(jnp.where(vt == m, blk, float(LANES)), axis=0, keepdims=True)
        hit = blk == first
        bias = jnp.where(hit, 0.0, bias)
        vt = jnp.where(hit, -jnp.inf, vt)
    return bias.T


def _rep(m, n):
    return jnp.concatenate([m] * n, axis=1) if n > 1 else m


def _rowmax(s):
    return jnp.broadcast_to(jnp.max(s, axis=-1, keepdims=True), (s.shape[0], LANES))


def _rowsum(s):
    return jnp.broadcast_to(jnp.sum(s, axis=-1, keepdims=True), (s.shape[0], LANES))


def _mask_heads(s, mask, rows, fill):
    n, c = s.shape[0] // rows, s.shape[1]
    return jnp.where(mask[None], s.reshape(n, rows, c), fill).reshape(n * rows, c)


def _gate_mix(gates, o_cmp, o_sel, o_win, rows):
    n_heads = o_cmp.shape[0] // rows
    hpg = n_heads // N_KV
    parts = []
    for h in range(n_heads):
        g = h // hpg
        lanes = slice(g * HEAD_DIM, (g + 1) * HEAD_DIM)
        rs = slice(h * rows, (h + 1) * rows)
        parts.append(gates[:, 3 * h:3 * h + 1] * o_cmp[rs, lanes]
                     + gates[:, 3 * h + 1:3 * h + 2] * o_sel[rs, lanes]
                     + gates[:, 3 * h + 2:3 * h + 3] * o_win[rs, lanes])
    return jnp.concatenate(parts, axis=1)


def _cmp_attend(qc, kc, vc, valid, n_heads, rows, transposed=False):
    nc = kc.shape[1] if transposed else kc.shape[0]
    if transposed:
        s = jnp.dot(qc, kc, preferred_element_type=F32)
    else:
        s = lax.dot_general(qc, kc, _NT, preferred_element_type=F32)
    if valid is not None:
        s = _mask_heads(s, valid, rows, NEG)
    e = jnp.exp(s - _rep(_rowmax(s), nc // LANES))
    p = e / _rep(_rowsum(e), nc // LANES)
    if valid is not None:
        p = _mask_heads(p, valid, rows, 0.0)
    pb = p.astype(BF16)
    if transposed:
        o = lax.dot_general(pb, vc, _NT, preferred_element_type=F32)
    else:
        o = jnp.dot(pb, vc, preferred_element_type=F32)
    hpg = n_heads // N_KV
    imps = []
    for g in range(N_KV):
        ph = p[g * hpg * rows:(g * hpg + 1) * rows]
        for h in range(1, hpg):
            ph = ph + p[(g * hpg + h) * rows:(g * hpg + h + 1) * rows]
        imps.append(ph[:, :nc // 2] + ph[:, nc // 2:])
    return o, imps


_ROW_SPLIT = 2


def _attn_prompt_body(q_ref, g_ref, kc_ref, vc_ref, ksa_ref, vsa_ref, kwb_ref, vwa_ref, o_ref,
                      qa_ref, m_ref, acc_ref, s_ref, *, tq, tk, n_heads):
    t0 = pl.program_id(1) * tq
    rows = n_heads * tq
    hpg = n_heads // N_KV
    hr = rows // _ROW_SPLIT
    for h in range(n_heads):
        qa_ref[h * tq:(h + 1) * tq, 0:LANES] = q_ref[:, h * LANES:(h + 1) * LANES]
    qc = qa_ref[:, 0:LANES]
    qpos = t0 + lax.broadcasted_iota(jnp.int32, (tq, 1), 0)

    nc = kc_ref.shape[0]
    col = lax.broadcasted_iota(jnp.int32, (tq, nc), 1)
    cblk = jnp.where(col < nc // 2, 2 * col, 2 * (col - nc // 2) + 1)
    valid = (cblk + 1) * L_CMP - 1 <= qpos
    o_cmp, imps = _cmp_attend(qc, kc_ref[...], vc_ref[...], valid, n_heads, tq)

    blk = lax.broadcasted_iota(jnp.int32, (tq, LANES), 1)
    cur = qpos // L_SEL
    forced = (blk == 0) | (blk == cur) | (blk == cur - 1)
    bonus = jnp.where(forced, FORCE, jnp.where(blk * L_SEL <= qpos, 0.0, NEG))
    bias = _select_bias(jnp.concatenate([imp + bonus for imp in imps], axis=0), N_SEL)
    for h in range(n_heads):
        g = h // hpg
        qa_ref[h * tq:(h + 1) * tq, LANES:2 * LANES] = bias[g * tq:(g + 1) * tq].astype(BF16)

    span = WINDOW + tq
    w0 = pl.multiple_of(jnp.maximum(t0 - WINDOW, 0), tq)
    kposw = w0 + lax.broadcasted_iota(jnp.int32, (tq, span), 1)
    band = (kposw <= qpos) & (kposw > qpos - WINDOW)
    o_wins = []
    for r in range(_ROW_SPLIT):
        s = lax.dot_general(qa_ref[r * hr:(r + 1) * hr, 0:LANES], kwb_ref[pl.ds(w0, span), :], _NT,
                            preferred_element_type=F32)
        s = _mask_heads(s, band, tq, NEG)
        p = jnp.exp(s - _rep(_rowmax(s), span // LANES)).astype(BF16)
        accw = jnp.dot(p, vwa_ref[pl.ds(w0, span), :], preferred_element_type=F32)
        o_wins.append(accw[:, 0:LANES] / accw[:, LANES:2 * LANES])
    o_win = jnp.concatenate(o_wins, axis=0)

    m_ref[...] = jnp.full(m_ref.shape, -jnp.inf, F32)
    acc_ref[...] = jnp.zeros(acc_ref.shape, F32)

    def tile(kt, causal):
        ks0 = pl.multiple_of(kt * tk, tk)
        for r in range(_ROW_SPLIT):
            rs = slice(r * hr, (r + 1) * hr)
            s_ref[rs, :] = lax.dot_general(qa_ref[rs, :], ksa_ref[pl.ds(ks0, tk), :], _NT,
                                           preferred_element_type=F32)
        for r in range(_ROW_SPLIT):
            rs = slice(r * hr, (r + 1) * hr)
            s = s_ref[rs, :]
            if causal:
                kpos = ks0 + lax.broadcasted_iota(jnp.int32, (tq, tk), 1)
                s = _mask_heads(s, kpos <= qpos, tq, NEG)
            m_prev = m_ref[rs, :]
            m_new = jnp.maximum(m_prev, jnp.max(s, axis=-1, keepdims=True))
            alpha = jnp.exp(m_prev - m_new)
            p = jnp.exp(s - _rep(m_new, tk // LANES)).astype(BF16)
            pv = jnp.dot(p, vsa_ref[pl.ds(ks0, tk), :], preferred_element_type=F32)
            acc_ref[rs, :] = acc_ref[rs, :] * _rep(alpha, 2) + pv
            m_ref[rs, :] = m_new

    n_full = t0 // tk

    def full_tile(kt, carry):
        tile(kt, False)
        return carry

    lax.fori_loop(0, n_full, full_tile, 0)
    tile(n_full, True)
    acc = acc_ref[...]
    o_sel = acc[:, 0:LANES] / acc[:, LANES:2 * LANES]

    o_ref[...] = _gate_mix(g_ref[...], o_cmp, o_sel, o_win, tq).astype(BF16)


def _attn_prompt(q, gates, kc, vc, ksa, vsa, kwb, vwa, *, tq, tk):
    batch, seq, qw = q.shape
    n_heads = qw // LANES
    per_b = lambda a: pl.BlockSpec((None,) + a.shape[1:], lambda b, i: (b,) + (0,) * (a.ndim - 1))
    row = lambda c: pl.BlockSpec((None, tq, c), lambda b, i: (b, i, 0))
    return pl.pallas_call(
        functools.partial(_attn_prompt_body, tq=tq, tk=tk, n_heads=n_heads),
        out_shape=jax.ShapeDtypeStruct((batch, seq, n_heads * HEAD_DIM), BF16),
        grid=(batch, seq // tq),
        in_specs=[row(qw), row(LANES), per_b(kc), per_b(vc), per_b(ksa), per_b(vsa), per_b(kwb), per_b(vwa)],
        out_specs=row(n_heads * HEAD_DIM),
        scratch_shapes=[pltpu.VMEM((n_heads * tq, 2 * LANES), BF16),
                        pltpu.VMEM((n_heads * tq, LANES), F32),
                        pltpu.VMEM((n_heads * tq, 2 * LANES), F32),
                        pltpu.VMEM((n_heads * tq, tk), F32)],
        compiler_params=_params(("arbitrary", "arbitrary")),
        name="attn_prompt",
    )(q, gates, kc, vc, ksa, vsa, kwb, vwa)


_TP = SUBLANES


def _attn_sample_body(pt_ref, q_ref, g_ref, new_ref, swk_ref, swv_ref, wck_ref, wcv_ref,
                      cck_hbm, ccv_hbm, csk_hbm, csv_hbm, o_ref,
                      bufs, sems, cs_ref, ka_ref, va_ref, kn_ref, vn_ref, kwn_ref, vwn_ref, vw_ref, qa_ref,
                      *, n_heads, n_tok, n_pages, page):
    i = pl.program_id(0)
    n_seq = pl.num_programs(0)
    caches = (cck_hbm, ccv_hbm, csk_hbm, csv_hbm)
    hpg = n_heads // N_KV
    rows = n_heads * _TP
    past = n_pages * page
    wb = swk_ref.shape[1]

    def page_copy(a, seq, p):
        return pltpu.make_async_copy(caches[a].at[pt_ref[seq, p]], bufs.at[a, p], sems.at[a])

    def start(arrays, seq):
        def body(p, c):
            for a in arrays:
                page_copy(a, seq, p).start()
            return c
        lax.fori_loop(0, n_pages, body, 0)

    def wait(a):
        def body(p, c):
            page_copy(a, 0, p).wait()
            return c
        lax.fori_loop(0, n_pages, body, 0, unroll=True)

    def cast_pages(a, dst):
        for p in range(n_pages):
            dst[0:KV_W, p * page:(p + 1) * page] = bufs[a, p].astype(BF16)

    @pl.when(i == 0)
    def _():
        start((0, 1), 0)
        start((2, 3), 0)
        blk = lax.broadcasted_iota(jnp.int32, (LANES, past), 0)
        kpos = lax.broadcasted_iota(jnp.int32, (LANES, past), 1)
        ka_ref[KV_W:, :] = jnp.where(kpos // L_SEL == blk, 1.0, 0.0).astype(BF16)
        va_ref[KV_W:, :] = jnp.ones((LANES, past), BF16)
        vw_ref[KV_W:, :] = jnp.ones((LANES, wb), BF16)
        kn_ref[...] = jnp.zeros(kn_ref.shape, BF16)
        vn_ref[...] = jnp.zeros(vn_ref.shape, BF16)
        kwn_ref[...] = jnp.zeros(kwn_ref.shape, BF16)
        vwn_ref[...] = jnp.zeros(vwn_ref.shape, BF16)
        qa_ref[...] = jnp.zeros(qa_ref.shape, BF16)

    for h in range(n_heads):
        qa_ref[h * _TP:h * _TP + n_tok, 0:LANES] = q_ref[:, h * LANES:(h + 1) * LANES]
    qc = qa_ref[:, 0:LANES]
    tok = lax.broadcasted_iota(jnp.int32, (_TP, 1), 0)

    wait(0)
    wait(1)
    cast_pages(0, cs_ref.at[0])
    cast_pages(1, cs_ref.at[1])

    @pl.when(i + 1 < n_seq)
    def _():
        start((0, 1), i + 1)

    kct = jnp.dot(cs_ref[0], wck_ref[...], preferred_element_type=F32).astype(BF16)
    vct = jnp.dot(cs_ref[1], wcv_ref[...], preferred_element_type=F32).astype(BF16)
    o_cmp, imps = _cmp_attend(qc, kct, vct, None, n_heads, _TP, transposed=True)

    n_blk = past // L_SEL
    blk = lax.broadcasted_iota(jnp.int32, (_TP, LANES), 1)
    bonus = jnp.where((blk == 0) | (blk == n_blk - 1), FORCE, 0.0)
    v = jnp.concatenate([imp + bonus for imp in imps]
                        + [jnp.full((LANES - N_KV * _TP, LANES), NEG, F32)], axis=0)
    bias = _select_bias(v, N_SEL - 1)
    for h in range(n_heads):
        g = h // hpg
        qa_ref[h * _TP:(h + 1) * _TP, LANES:] = bias[g * _TP:(g + 1) * _TP].astype(BF16)

    wait(2)
    wait(3)
    cast_pages(2, ka_ref)
    cast_pages(3, va_ref)

    @pl.when(i + 1 < n_seq)
    def _():
        start((2, 3), i + 1)

    new = new_ref[...]
    kn_ref[0:n_tok, :] = new[:, 2 * KV_W:3 * KV_W].astype(BF16)
    vn_ref[0:n_tok, 0:LANES] = new[:, 3 * KV_W:4 * KV_W].astype(BF16)
    vn_ref[0:n_tok, LANES:] = jnp.ones((n_tok, LANES), BF16)
    jn = lax.broadcasted_iota(jnp.int32, (_TP, LANES), 1)
    new_ok = (jn <= tok) & (jn < n_tok)
    s_past = jnp.dot(qa_ref[...], ka_ref[...], preferred_element_type=F32)
    s_new = lax.dot_general(qc, kn_ref[...], _NT, preferred_element_type=F32)
    s_new = _mask_heads(s_new, new_ok, _TP, NEG)
    m = jnp.maximum(_rowmax(s_past), _rowmax(s_new))
    acc = (lax.dot_general(jnp.exp(s_past - _rep(m, past // LANES)).astype(BF16), va_ref[...], _NT,
                           preferred_element_type=F32)
           + jnp.dot(jnp.exp(s_new - m).astype(BF16), vn_ref[...], preferred_element_type=F32))
    o_sel = acc[:, 0:LANES] / acc[:, LANES:]

    kwn_ref[0:n_tok, :] = new[:, 4 * KV_W:5 * KV_W].astype(BF16)
    vwn_ref[0:n_tok, 0:LANES] = new[:, 5 * KV_W:6 * KV_W].astype(BF16)
    vwn_ref[0:n_tok, LANES:] = jnp.ones((n_tok, LANES), BF16)
    vw_ref[0:KV_W, :] = swv_ref[...].astype(BF16)
    s_wp = jnp.dot(qc, swk_ref[...].astype(BF16), preferred_element_type=F32)
    s_wn = lax.dot_general(qc, kwn_ref[...], _NT, preferred_element_type=F32)
    c = lax.broadcasted_iota(jnp.int32, (_TP, wb), 1)
    s_wp = _mask_heads(s_wp, c - wb > tok - WINDOW, _TP, NEG)
    s_wn = _mask_heads(s_wn, new_ok, _TP, NEG)
    m = jnp.maximum(_rowmax(s_wp), _rowmax(s_wn))
    accw = (lax.dot_general(jnp.exp(s_wp - _rep(m, wb // LANES)).astype(BF16), vw_ref[...], _NT,
                            preferred_element_type=F32)
            + jnp.dot(jnp.exp(s_wn - m).astype(BF16), vwn_ref[...], preferred_element_type=F32))
    o_win = accw[:, 0:LANES] / accw[:, LANES:]

    gates = jnp.concatenate([g_ref[...], jnp.zeros((_TP - n_tok, LANES), F32)], axis=0)
    o_ref[...] = _gate_mix(gates, o_cmp, o_sel, o_win, _TP)[0:n_tok].astype(BF16)


def _attn_sample(page_table, q, gates, new, swk, swv, wck, wcv, cck, ccv, csk, csv):
    n_seq, n_tok, qw = q.shape
    n_heads = qw // LANES
    n_pages = page_table.shape[1]
    page = cck.shape[2]
    past = n_pages * page
    wb = swk.shape[2]
    per_seq = lambda a: pl.BlockSpec((None,) + a.shape[1:], lambda i, pt: (i,) + (0,) * (a.ndim - 1))
    hbm = pl.BlockSpec(memory_space=pl.ANY)
    grid_spec = pltpu.PrefetchScalarGridSpec(
        num_scalar_prefetch=1,
        grid=(n_seq,),
        in_specs=[per_seq(q), per_seq(gates), per_seq(new), per_seq(swk), per_seq(swv),
                  _resident(wck.shape), _resident(wcv.shape), hbm, hbm, hbm, hbm],
        out_specs=pl.BlockSpec((None, n_tok, n_heads * HEAD_DIM), lambda i, pt: (i, 0, 0)),
        scratch_shapes=[pltpu.VMEM((4, n_pages, KV_W, page), F32),
                        pltpu.SemaphoreType.DMA((4,)),
                        pltpu.VMEM((2, KV_W, past), BF16),
                        pltpu.VMEM((2 * LANES, past), BF16),
                        pltpu.VMEM((2 * LANES, past), BF16),
                        pltpu.VMEM((LANES, KV_W), BF16),
                        pltpu.VMEM((LANES, 2 * LANES), BF16),
                        pltpu.VMEM((LANES, KV_W), BF16),
                        pltpu.VMEM((LANES, 2 * LANES), BF16),
                        pltpu.VMEM((2 * LANES, wb), BF16),
                        pltpu.VMEM((n_heads * _TP, 2 * LANES), BF16)])
    return pl.pallas_call(
        functools.partial(_attn_sample_body, n_heads=n_heads, n_tok=n_tok, n_pages=n_pages, page=page),
        out_shape=jax.ShapeDtypeStruct((n_seq, n_tok, n_heads * HEAD_DIM), BF16),
        grid_spec=grid_spec,
        compiler_params=_params(("arbitrary",)),
        name="attn_sample",
    )(page_table, q, gates, new, swk, swv, wck, wcv, cck, ccv, csk, csv)


def _cmp_placement(w, past):
    nc = past // L_CMP
    k = jnp.arange(past)
    b = k // L_CMP
    col = b // 2 + (b % 2) * (nc // 2)
    return jnp.where(col[:, None] == jnp.arange(nc)[None, :], w[k % L_CMP][:, None], 0.0).astype(BF16)


def _out_ffn_body(on_ref, yp_ref, x_ref, g2_ref, sh_ref, sc_ref, gt_ref, gains_ref, wout_ref, wi_ref, wo_ref,
                  o_ref, act_ref, *, d_ff):
    half = on_ref.shape[1]
    y = (jnp.dot(on_ref[...], wout_ref[0:half, :], preferred_element_type=F32)
         + jnp.dot(yp_ref[...], wout_ref[half:, :], preferred_element_type=F32))
    x = x_ref[...] + g2_ref[...] * _rms(y, gains_ref[0:1, :])
    h = (_rms(x, gains_ref[1:2, :]) * (1 + sc_ref[...]) + sh_ref[...]).astype(BF16)
    y = _swiglu(h, wi_ref, wo_ref, act_ref, d_ff)
    o_ref[...] = x + 0.5 * gt_ref[...] * _rms(y, gains_ref[2:3, :])


def _out_ffn(o_nsa, y_pool, x, mods, gains, w_out, wi, wo, *, tm, per_row, tiles_per_seq):
    r, d = x.shape
    d_ff = wo.shape[0]
    ms = _mod_spec(per_row, tm, d, tiles_per_seq)
    row = lambda c: pl.BlockSpec((tm, c), lambda i: (i, 0))
    return pl.pallas_call(
        functools.partial(_out_ffn_body, d_ff=d_ff),
        out_shape=jax.ShapeDtypeStruct((r, d), F32),
        grid=(r // tm,),
        in_specs=[row(o_nsa.shape[1]), row(y_pool.shape[1]), row(d), ms, ms, ms, ms,
                  _resident((3, d)), _resident(w_out.shape), _resident(wi.shape), _resident(wo.shape)],
        out_specs=row(d),
        scratch_shapes=[pltpu.VMEM((tm, d_ff), BF16)],
        compiler_params=_params(("arbitrary",)),
        name="out_ffn",
    )(o_nsa, y_pool, x, *mods, gains, w_out, wi, wo)


def _even_odd(a):
    return jnp.concatenate([a[:, 0::2], a[:, 1::2]], axis=1).astype(BF16)


def kernel(x_prompt, x_sample, cache_cmp_k, cache_cmp_v, cache_sel_k, cache_sel_v, state_win_k, state_win_v,
           state_pool, page_table, c_prompt, c_sample, w_ada, b_ada, norm_gains, ffn1_wi, ffn1_wo, ffn2_wi,
           ffn2_wo, w_in, w_out, cmp_w, pool_w, pool_scale):
    depth = w_ada.shape[0]
    batch, seq, d = x_prompt.shape
    n_seq, n_tok, _ = x_sample.shape
    n_pages = page_table.shape[1]
    page = cache_cmp_k.shape[2]
    past = n_pages * page
    tm, tq, tk = 512, 128, 512
    assert depth == 1 and seq % tm == 0 and seq // L_SEL == LANES and past // L_SEL == LANES
    assert n_tok <= _TP and past % L_SEL == 0 and past >= max(POOL_WINDOWS) and seq >= WINDOW + tq
    assert state_pool.shape[2] == max(POOL_WINDOWS) - 1 and cache_cmp_k.shape[3:] == (N_KV, HEAD_DIM)
    l = 0
    n_rows_s = n_seq * n_tok
    tps = seq // tm

    mod = _modulation(jnp.concatenate([c_prompt, c_sample], axis=0), w_ada[l], b_ada[l])
    mod_p = [mod[:batch, k][:, None, :] for k in range(N_MOD)]
    mod_s = [jnp.tile(mod[batch:, k], (n_tok, 1)) for k in range(N_MOD)]
    gains = norm_gains[l]
    wi1, wo1 = ffn1_wi[l].astype(BF16), ffn1_wo[l].astype(BF16)
    wi2, wo2 = ffn2_wi[l].astype(BF16), ffn2_wo[l].astype(BF16)
    w_all = _prep_w_in(w_in[l])
    w_o = w_out[l].astype(BF16)
    pw = pool_w[l].astype(BF16)
    ps = pool_scale[l].reshape(1, -1)
    cmp_rows_p = jnp.tile(cmp_w[l][:, :, None], (1, tm // L_CMP, KV_W))

    xp = x_prompt.reshape(batch * seq, d)
    xs = jnp.swapaxes(x_sample, 0, 1).reshape(n_rows_s, d)

    xp = _ffn(xp, mod_p[0:3], gains[0:2], wi1, wo1, tm=tm, per_row=False, tiles_per_seq=tps)
    xs = _ffn(xs, mod_s[0:3], gains[0:2], wi1, wo1, tm=n_rows_s, per_row=True, tiles_per_seq=1)

    (kcr, vcr, ks, vs, kw, vw, kc, vc, qp, gp, ksa, vsa, kwb, vwa, yp_p, u_tail) = _inproj_prompt(
        xp, mod_p[3], mod_p[4], gains[2:3], w_all, cmp_rows_p, pw, ps, batch=batch, seq=seq, tm=tm)
    on_p = _attn_prompt(qp, gp, _even_odd(kc), _even_odd(vc), ksa, vsa, kwb, vwa, tq=tq, tk=tk)

    hist = jnp.swapaxes(state_pool[l], 0, 1)
    kv_s, q_s, g_s, yp_s, u_s = _inproj_sample(
        xs, mod_s[3], mod_s[4], gains[2:3], w_all, hist, pw, ps, n_seq=n_seq, n_tok=n_tok, past_len=past)
    by_seq = lambda a: jnp.swapaxes(a.reshape(n_tok, n_seq, a.shape[-1]), 0, 1)
    new_s = by_seq(kv_s)
    cols = lambda a: jnp.transpose(a, (0, 2, 3, 1)).reshape(a.shape[0], KV_W, a.shape[1])
    on_s = _attn_sample(page_table, by_seq(q_s), by_seq(g_s), new_s, cols(state_win_k[l]), cols(state_win_v[l]),
                        _cmp_placement(cmp_w[l, 0], past), _cmp_placement(cmp_w[l, 1], past),
                        cols(cache_cmp_k[l]), cols(cache_cmp_v[l]), cols(cache_sel_k[l]), cols(cache_sel_v[l]))
    on_s = jnp.swapaxes(on_s, 0, 1).reshape(n_rows_s, -1)

    xp = _out_ffn(on_p.reshape(batch * seq, -1), yp_p.reshape(batch * seq, -1), xp,
                  [mod_p[5], mod_p[6], mod_p[7], mod_p[8]], gains[3:6], w_o, wi2, wo2,
                  tm=tm, per_row=False, tiles_per_seq=tps)
    xs = _out_ffn(on_s, yp_s, xs, [mod_s[5], mod_s[6], mod_s[7], mod_s[8]], gains[3:6], w_o, wi2, wo2,
                  tm=n_rows_s, per_row=True, tiles_per_seq=1)

    heads = lambda a: a.reshape(a.shape[:-1] + (N_KV, HEAD_DIM))[None]
    keep = min(WINDOW, seq)
    n_state = state_pool.shape[2]
    y_prompt = xp.reshape(batch, seq, d)
    y_sample = jnp.swapaxes(xs.reshape(n_tok, n_seq, d), 0, 1)
    p_out = (heads(kcr), heads(vcr), heads(ks), heads(vs), heads(kw[:, seq - keep:]), heads(vw[:, seq - keep:]),
             u_tail[:, _HALO - n_state:][None])
    new4 = [heads(new_s[..., j * KV_W:(j + 1) * KV_W]) for j in range(6)]
    keep_s = min(WINDOW, state_win_k.shape[2] + n_tok)
    win_k = jnp.concatenate([state_win_k, new4[4]], axis=2)
    win_v = jnp.concatenate([state_win_v, new4[5]], axis=2)
    pool_s = jnp.concatenate([state_pool[l], by_seq(u_s)], axis=1)
    s_out = (new4[0], new4[1], new4[2], new4[3], win_k[:, :, win_k.shape[2] - keep_s:],
             win_v[:, :, win_v.shape[2] - keep_s:], pool_s[:, pool_s.shape[1] - n_state:][None])
    return (y_prompt, y_sample) + p_out + s_out
```

```python
import functools

import jax
import jax.numpy as jnp
from jax import lax
from jax.experimental import pallas as pl
from jax.experimental.pallas import tpu as pltpu

HEAD_DIM = 64
N_KV = 2
L_CMP = 32
L_SEL = 64
N_SEL = 16
WINDOW = 512
POOL_WINDOWS = (2, 4, 8, 16)
N_MOD = 9
RMS_EPS = 1e-6
NEG = -1e30

LANES = 128
SUBLANES = 8
VMEM_LIMIT_BYTES = 56 * 1024 * 1024

F32 = jnp.float32
BF16 = jnp.bfloat16

KV_W = N_KV * HEAD_DIM
_NT = (((1,), (1,)), ((), ()))


def _rms(x, g):
    return x * lax.rsqrt(jnp.mean(x * x, axis=-1, keepdims=True) + RMS_EPS) * g


def _silu(a):
    return a * jax.nn.sigmoid(a)


def _params(sem):
    return pltpu.CompilerParams(dimension_semantics=sem, vmem_limit_bytes=VMEM_LIMIT_BYTES)


def _resident(shape):
    return pl.BlockSpec(shape, lambda *_: (0,) * len(shape), pipeline_mode=pl.Buffered(1))


def _mod_body(c_ref, w_ref, b_ref, o_ref):
    a = _silu(c_ref[...]).astype(BF16)
    o_ref[...] = jnp.dot(a, w_ref[...].astype(BF16), preferred_element_type=F32) + b_ref[...]


def _modulation(c, w_ada, b_ada):
    n, d = c.shape
    n_pad = -(-n // SUBLANES) * SUBLANES
    c = jnp.pad(c, ((0, n_pad - n), (0, 0)))
    out = pl.pallas_call(
        _mod_body,
        out_shape=jax.ShapeDtypeStruct((n_pad, N_MOD * d), F32),
        grid=(N_MOD,),
        in_specs=[pl.BlockSpec((n_pad, d), lambda j: (0, 0)),
                  pl.BlockSpec((d, d), lambda j: (0, j)),
                  pl.BlockSpec((1, d), lambda j: (0, j))],
        out_specs=pl.BlockSpec((n_pad, d), lambda j: (0, j)),
        compiler_params=_params(("arbitrary",)),
        name="adaln_modulation",
    )(c, w_ada, b_ada.reshape(1, -1))
    return out[:n].reshape(n, N_MOD, d)


def _ffn_chunks(d_ff):
    chunks, o = [], 0
    while o < d_ff:
        w = min(512, d_ff - o)
        chunks.append((o, w))
        o += w
    return chunks


def _swiglu(h, wi_ref, wo_ref, act_ref, d_ff):
    for o, w in _ffn_chunks(d_ff):
        a = jnp.dot(h, wi_ref[:, o:o + w], preferred_element_type=F32)
        b = jnp.dot(h, wi_ref[:, d_ff + o:d_ff + o + w], preferred_element_type=F32)
        act_ref[:, o:o + w] = (_silu(a) * b).astype(BF16)
    return jnp.dot(act_ref[...], wo_ref[...], preferred_element_type=F32)


def _ffn_body(x_ref, sh_ref, sc_ref, gt_ref, gains_ref, wi_ref, wo_ref, o_ref, act_ref, *, d_ff):
    x = x_ref[...]
    h = (_rms(x, gains_ref[0:1, :]) * (1 + sc_ref[...]) + sh_ref[...]).astype(BF16)
    y = _swiglu(h, wi_ref, wo_ref, act_ref, d_ff)
    o_ref[...] = x + 0.5 * gt_ref[...] * _rms(y, gains_ref[1:2, :])


def _mod_spec(per_row, tm, d, tiles_per_seq):
    if per_row:
        return pl.BlockSpec((tm, d), lambda i: (i, 0))
    return pl.BlockSpec((None, 1, d), lambda i: (i // tiles_per_seq, 0, 0))


def _ffn(x, mods, gains, wi, wo, *, tm, per_row, tiles_per_seq):
    r, d = x.shape
    d_ff = wo.shape[0]
    ms = _mod_spec(per_row, tm, d, tiles_per_seq)
    return pl.pallas_call(
        functools.partial(_ffn_body, d_ff=d_ff),
        out_shape=jax.ShapeDtypeStruct((r, d), F32),
        grid=(r // tm,),
        in_specs=[pl.BlockSpec((tm, d), lambda i: (i, 0)), ms, ms, ms,
                  _resident((2, d)), _resident(wi.shape), _resident(wo.shape)],
        out_specs=pl.BlockSpec((tm, d), lambda i: (i, 0)),
        scratch_shapes=[pltpu.VMEM((tm, d_ff), BF16)],
        compiler_params=_params(("arbitrary",)),
        name="ffn",
    )(x, *mods, gains, wi, wo)


_QW = 8 * LANES
_O_KC, _O_VC, _O_KS, _O_VS, _O_KW, _O_VW = (_QW + i * KV_W for i in range(6))
_O_G = _QW + 6 * KV_W
_O_U = _O_G + LANES
_POOL_W = len(POOL_WINDOWS) * LANES
_IN_COLS = _O_U + _POOL_W
_HALO = 16


def _prep_w_in(w_in):
    d = w_in.shape[0]
    n_heads = (_QW // LANES)
    hpg = n_heads // N_KV
    q = w_in[:, :n_heads * HEAD_DIM].reshape(d, n_heads, 1, HEAD_DIM)
    own_group = (jnp.arange(n_heads) // hpg)[:, None] == jnp.arange(N_KV)[None, :]
    slot = jnp.where(own_group[None, :, :, None], q, 0.0)
    o = n_heads * HEAD_DIM
    kv = w_in[:, o:o + 6 * KV_W]
    o += 6 * KV_W
    n_g = n_heads * 3
    g = jnp.pad(w_in[:, o:o + n_g], ((0, 0), (0, LANES - n_g)))
    u = w_in[:, o + n_g:]
    return jnp.concatenate([slot.reshape(d, _QW), kv, g, u], axis=1).astype(BF16)


def _project(x_ref, sh_ref, sc_ref, gain_ref, w_ref):
    h = (_rms(x_ref[...], gain_ref[...]) * (1 + sc_ref[...]) + sh_ref[...]).astype(BF16)
    return jnp.dot(h, w_ref[...], preferred_element_type=F32)


def _pool_out(d, pw_ref, ps_ref):
    parts = [jnp.dot(d[:, g * LANES:(g + 1) * LANES].astype(BF16), pw_ref[g], preferred_element_type=F32)
             for g in range(len(POOL_WINDOWS))]
    return (jnp.concatenate(parts, axis=1) * ps_ref[...]).astype(BF16)


def _inproj_prompt_body(x_ref, sh_ref, sc_ref, gain_ref, w_ref, cw_ref, pw_ref, ps_ref,
                        kcr_ref, vcr_ref, ks_ref, vs_ref, kw_ref, vw_ref, kc_ref, vc_ref,
                        q_ref, g_ref, ksa_ref, vsa_ref, kwb_ref, vwa_ref, yp_ref, ul_ref,
                        ext_ref, *, tm):
    i = pl.program_id(1)
    z = _project(x_ref, sh_ref, sc_ref, gain_ref, w_ref)
    kcr, vcr = z[:, _O_KC:_O_KC + KV_W], z[:, _O_VC:_O_VC + KV_W]
    ks, vs = z[:, _O_KS:_O_KS + KV_W], z[:, _O_VS:_O_VS + KV_W]
    kw, vw = z[:, _O_KW:_O_KW + KV_W], z[:, _O_VW:_O_VW + KV_W]
    kcr_ref[...], vcr_ref[...] = kcr, vcr
    ks_ref[...], vs_ref[...] = ks, vs
    kw_ref[...], vw_ref[...] = kw, vw
    kc_ref[...] = (kcr * cw_ref[0]).reshape(tm // L_CMP, L_CMP, KV_W).sum(axis=1)
    vc_ref[...] = (vcr * cw_ref[1]).reshape(tm // L_CMP, L_CMP, KV_W).sum(axis=1)
    q_ref[...] = (z[:, :_QW] * (HEAD_DIM ** -0.5)).astype(BF16)
    g_ref[...] = jax.nn.sigmoid(z[:, _O_G:_O_G + LANES])
    pos = i * tm + lax.broadcasted_iota(jnp.int32, (tm, LANES), 0)
    blk = lax.broadcasted_iota(jnp.int32, (tm, LANES), 1)
    onehot = jnp.where(pos // L_SEL == blk, 1.0, 0.0).astype(BF16)
    ones = jnp.ones((tm, LANES), BF16)
    ksa_ref[...] = jnp.concatenate([ks.astype(BF16), onehot], axis=1)
    vsa_ref[...] = jnp.concatenate([vs.astype(BF16), ones], axis=1)
    kwb_ref[...] = kw.astype(BF16)
    vwa_ref[...] = jnp.concatenate([vw.astype(BF16), ones], axis=1)
    u = z[:, _O_U:_O_U + _POOL_W]

    @pl.when(i == 0)
    def _():
        ext_ref[0:_HALO, :] = jnp.zeros((_HALO, _POOL_W), F32)

    ext_ref[_HALO:, :] = u
    tpos = i * tm + lax.broadcasted_iota(jnp.int32, (tm, LANES), 0)
    diffs = []
    for g, w in enumerate(POOL_WINDOWS):
        s = ext_ref[:, g * LANES:(g + 1) * LANES]
        k = 1
        while k < w:
            s = s + pltpu.roll(s, k, 0)
            k *= 2
        cnt = jnp.minimum(w, tpos + 1).astype(F32)
        diffs.append(s[_HALO:] / cnt - u[:, g * LANES:(g + 1) * LANES])
    yp_ref[...] = _pool_out(jnp.concatenate(diffs, axis=1), pw_ref, ps_ref)
    tail = u[tm - _HALO:, :]
    ext_ref[0:_HALO, :] = tail
    ul_ref[...] = tail


def _inproj_prompt(x, sh, sc, gain, w_all, cmp_rows, pool_w, pool_scale, *, batch, seq, tm):
    d = x.shape[1]
    tps = seq // tm
    row = lambda c: pl.BlockSpec((None, tm, c), lambda b, i: (b, i, 0))
    mod = pl.BlockSpec((None, 1, d), lambda b, i: (b, 0, 0))
    f32o = lambda c: jax.ShapeDtypeStruct((batch, seq, c), F32)
    bf16o = lambda c: jax.ShapeDtypeStruct((batch, seq, c), BF16)
    out_shape = ([f32o(KV_W)] * 6
                 + [jax.ShapeDtypeStruct((batch, seq // L_CMP, KV_W), F32)] * 2
                 + [bf16o(_QW), f32o(LANES), bf16o(2 * LANES), bf16o(2 * LANES), bf16o(KV_W), bf16o(2 * LANES),
                    bf16o(_POOL_W), jax.ShapeDtypeStruct((batch, _HALO, _POOL_W), F32)])
    out_specs = ([row(KV_W)] * 6
                 + [pl.BlockSpec((None, tm // L_CMP, KV_W), lambda b, i: (b, i, 0))] * 2
                 + [row(_QW), row(LANES), row(2 * LANES), row(2 * LANES), row(KV_W), row(2 * LANES),
                    row(_POOL_W), pl.BlockSpec((None, _HALO, _POOL_W), lambda b, i: (b, 0, 0))])
    return pl.pallas_call(
        functools.partial(_inproj_prompt_body, tm=tm),
        out_shape=out_shape,
        grid=(batch, tps),
        in_specs=[pl.BlockSpec((tm, d), lambda b, i: (b * tps + i, 0)), mod, mod,
                  _resident((1, d)), _resident(w_all.shape), _resident(cmp_rows.shape),
                  _resident(pool_w.shape), _resident(pool_scale.shape)],
        out_specs=out_specs,
        scratch_shapes=[pltpu.VMEM((_HALO + tm, _POOL_W), F32)],
        compiler_params=_params(("arbitrary", "arbitrary")),
        name="inproj_prompt",
    )(x, sh, sc, gain, w_all, cmp_rows, pool_w, pool_scale)


def _inproj_sample_body(x_ref, sh_ref, sc_ref, gain_ref, w_ref, hist_ref, pw_ref, ps_ref,
                        kv_ref, q_ref, g_ref, yp_ref, u_ref, *, n_seq, n_tok, past_len):
    z = _project(x_ref, sh_ref, sc_ref, gain_ref, w_ref)
    kv_ref[...] = z[:, _O_KC:_O_KC + 6 * KV_W]
    q_ref[...] = (z[:, :_QW] * (HEAD_DIM ** -0.5)).astype(BF16)
    g_ref[...] = jax.nn.sigmoid(z[:, _O_G:_O_G + LANES])
    u = z[:, _O_U:_O_U + _POOL_W]
    u_ref[...] = u
    n_hist = hist_ref.shape[0]
    slabs = [hist_ref[j] for j in range(n_hist)] + [u[t * n_seq:(t + 1) * n_seq] for t in range(n_tok)]
    rows = []
    for t in range(n_tok):
        diffs = []
        for g, w in enumerate(POOL_WINDOWS):
            lanes = slice(g * LANES, (g + 1) * LANES)
            s = slabs[n_hist + t][:, lanes]
            for j in range(1, w):
                s = s + slabs[n_hist + t - j][:, lanes]
            diffs.append(s / float(min(w, past_len + t + 1)) - slabs[n_hist + t][:, lanes])
        rows.append(jnp.concatenate(diffs, axis=1))
    yp_ref[...] = _pool_out(jnp.concatenate(rows, axis=0), pw_ref, ps_ref)


def _inproj_sample(x, sh, sc, gain, w_all, hist, pool_w, pool_scale, *, n_seq, n_tok, past_len):
    r, d = x.shape
    full = lambda a: pl.BlockSpec(a.shape, lambda i: (0,) * a.ndim)
    out_shape = [jax.ShapeDtypeStruct((r, 6 * KV_W), F32), jax.ShapeDtypeStruct((r, _QW), BF16),
                 jax.ShapeDtypeStruct((r, LANES), F32), jax.ShapeDtypeStruct((r, _POOL_W), BF16),
                 jax.ShapeDtypeStruct((r, _POOL_W), F32)]
    args = (x, sh, sc, gain, w_all, hist, pool_w, pool_scale)
    return pl.pallas_call(
        functools.partial(_inproj_sample_body, n_seq=n_seq, n_tok=n_tok, past_len=past_len),
        out_shape=out_shape,
        grid=(1,),
        in_specs=[full(a) for a in args],
        out_specs=[pl.BlockSpec(s.shape, lambda i: (0, 0)) for s in out_shape],
        compiler_params=_params(("arbitrary",)),
        name="inproj_sample",
    )(*args)


_N_FORCED = 3


def _select_bias(v, n_pick):
    vt = v.T
    blk = lax.broadcasted_iota(jnp.int32, vt.shape, 0).astype(F32)
    always = vt == jnp.inf
    bias = jnp.where(always, 0.0, NEG)
    vt = jnp.where(always, -jnp.inf, vt)
    for _ in range(n_pick):
        m = jnp.max(vt, axis=0, keepdims=True)
        first = jnp.min(jnp.where(vt == m, blk, float(LANES)), axis=0, keepdims=True)
        hit = blk == first
        bias = jnp.where(hit, 0.0, bias)
        vt = jnp.where(hit, -jnp.inf, vt)
    return bias.T


def _rep(m, n):
    return jnp.concatenate([m] * n, axis=1) if n > 1 else m


def _rowmax(s):
    return jnp.broadcast_to(jnp.max(s, axis=-1, keepdims=True), (s.shape[0], LANES))


def _rowsum(s):
    return jnp.broadcast_to(jnp.sum(s, axis=-1, keepdims=True), (s.shape[0], LANES))


def _mask_heads(s, mask, rows, fill):
    n, c = s.shape[0] // rows, s.shape[1]
    return jnp.where(mask[None], s.reshape(n, rows, c), fill).reshape(n * rows, c)


def _gate_mix(gates, o_cmp, o_sel, o_win, rows):
    n_heads = o_cmp.shape[0] // rows
    hpg = n_heads // N_KV
    parts = []
    for h in range(n_heads):
        g = h // hpg
        lanes = slice(g * HEAD_DIM, (g + 1) * HEAD_DIM)
        rs = slice(h * rows, (h + 1) * rows)
        parts.append(gates[:, 3 * h:3 * h + 1] * o_cmp[rs, lanes]
                     + gates[:, 3 * h + 1:3 * h + 2] * o_sel[rs, lanes]
                     + gates[:, 3 * h + 2:3 * h + 3] * o_win[rs, lanes])
    return jnp.concatenate(parts, axis=1)


def _cmp_attend(qc, kc, vc, valid, n_heads, rows, transposed=False):
    nc = kc.shape[1] if transposed else kc.shape[0]
    if transposed:
        s = jnp.dot(qc, kc, preferred_element_type=F32)
    else:
        s = lax.dot_general(qc, kc, _NT, preferred_element_type=F32)
    if valid is not None:
        s = _mask_heads(s, valid, rows, NEG)
    e = jnp.exp(s - _rep(_rowmax(s), nc // LANES))
    p = e / _rep(_rowsum(e), nc // LANES)
    if valid is not None:
        p = _mask_heads(p, valid, rows, 0.0)
    pb = p.astype(BF16)
    if transposed:
        o = lax.dot_general(pb, vc, _NT, preferred_element_type=F32)
    else:
        o = jnp.dot(pb, vc, preferred_element_type=F32)
    hpg = n_heads // N_KV
    imps = []
    for g in range(N_KV):
        ph = p[g * hpg * rows:(g * hpg + 1) * rows]
        for h in range(1, hpg):
            ph = ph + p[(g * hpg + h) * rows:(g * hpg + h + 1) * rows]
        imps.append(ph[:, :nc // 2] + ph[:, nc // 2:])
    return o, imps


_ROW_SPLIT = 2


def _attn_prompt_body(q_ref, g_ref, kc_ref, vc_ref, ksa_ref, vsa_ref, kwb_ref, vwa_ref, o_ref,
                      qa_ref, m_ref, acc_ref, s_ref, *, tq, tk, n_heads):
    t0 = pl.program_id(1) * tq
    rows = n_heads * tq
    hpg = n_heads // N_KV
    hr = rows // _ROW_SPLIT
    for h in range(n_heads):
        qa_ref[h * tq:(h + 1) * tq, 0:LANES] = q_ref[:, h * LANES:(h + 1) * LANES]
    qc = qa_ref[:, 0:LANES]
    qpos = t0 + lax.broadcasted_iota(jnp.int32, (tq, 1), 0)

    nc = kc_ref.shape[0]
    col = lax.broadcasted_iota(jnp.int32, (tq, nc), 1)
    cblk = jnp.where(col < nc // 2, 2 * col, 2 * (col - nc // 2) + 1)
    valid = (cblk + 1) * L_CMP - 1 <= qpos
    o_cmp, imps = _cmp_attend(qc, kc_ref[...], vc_ref[...], valid, n_heads, tq)

    blk = lax.broadcasted_iota(jnp.int32, (tq, LANES), 1)
    cur = qpos // L_SEL
    forced = (blk == 0) | (blk == cur) | (blk == cur - 1)
    started = blk * L_SEL <= qpos
    v = [jnp.where(forced, jnp.inf, jnp.where(started, imp, NEG)) for imp in imps]
    bias = _select_bias(jnp.concatenate(v, axis=0), N_SEL - _N_FORCED)
    for h in range(n_heads):
        g = h // hpg
        qa_ref[h * tq:(h + 1) * tq, LANES:2 * LANES] = bias[g * tq:(g + 1) * tq].astype(BF16)

    span = WINDOW + tq
    w0 = pl.multiple_of(jnp.maximum(t0 - WINDOW, 0), tq)
    kposw = w0 + lax.broadcasted_iota(jnp.int32, (tq, span), 1)
    band = (kposw <= qpos) & (kposw > qpos - WINDOW)
    o_wins = []
    for r in range(_ROW_SPLIT):
        s = lax.dot_general(qa_ref[r * hr:(r + 1) * hr, 0:LANES], kwb_ref[pl.ds(w0, span), :], _NT,
                            preferred_element_type=F32)
        s = _mask_heads(s, band, tq, NEG)
        p = jnp.exp(s - _rep(_rowmax(s), span // LANES)).astype(BF16)
        accw = jnp.dot(p, vwa_ref[pl.ds(w0, span), :], preferred_element_type=F32)
        o_wins.append(accw[:, 0:LANES] / accw[:, LANES:2 * LANES])
    o_win = jnp.concatenate(o_wins, axis=0)

    m_ref[...] = jnp.full(m_ref.shape, -jnp.inf, F32)
    acc_ref[...] = jnp.zeros(acc_ref.shape, F32)

    def scores(kt, slot):
        ks0 = pl.multiple_of(kt * tk, tk)
        s_ref[slot] = lax.dot_general(qa_ref[...], ksa_ref[pl.ds(ks0, tk), :], _NT, preferred_element_type=F32)

    def accumulate(kt, slot, causal):
        ks0 = pl.multiple_of(kt * tk, tk)
        s = s_ref[slot]
        if causal:
            kpos = ks0 + lax.broadcasted_iota(jnp.int32, (tq, tk), 1)
            s = _mask_heads(s, kpos <= qpos, tq, NEG)
        m_prev = m_ref[...]
        m_new = jnp.maximum(m_prev, jnp.max(s, axis=-1, keepdims=True))
        alpha = jnp.exp(m_prev - m_new)
        p = jnp.exp(s - _rep(m_new, tk // LANES)).astype(BF16)
        pv = jnp.dot(p, vsa_ref[pl.ds(ks0, tk), :], preferred_element_type=F32)
        acc_ref[...] = acc_ref[...] * _rep(alpha, 2) + pv
        m_ref[...] = m_new

    n_full = t0 // tk
    scores(0, 0)

    def tile_pair(jj, carry):
        j = 2 * jj
        scores(j + 1, 1)
        accumulate(j, 0, False)
        scores(j + 2, 0)
        accumulate(j + 1, 1, False)
        return carry

    lax.fori_loop(0, n_full // 2, tile_pair, 0)

    @pl.when(n_full % 2 == 1)
    def _():
        scores(n_full, 1)
        accumulate(n_full - 1, 0, False)
        accumulate(n_full, 1, True)

    @pl.when(n_full % 2 == 0)
    def _():
        accumulate(n_full, 0, True)
    acc = acc_ref[...]
    o_sel = acc[:, 0:LANES] / acc[:, LANES:2 * LANES]

    o_ref[...] = _gate_mix(g_ref[...], o_cmp, o_sel, o_win, tq).astype(BF16)


def _attn_prompt(q, gates, kc, vc, ksa, vsa, kwb, vwa, *, tq, tk):
    batch, seq, qw = q.shape
    n_heads = qw // LANES
    per_b = lambda a: pl.BlockSpec((None,) + a.shape[1:], lambda b, i: (b,) + (0,) * (a.ndim - 1))
    row = lambda c: pl.BlockSpec((None, tq, c), lambda b, i: (b, i, 0))
    return pl.pallas_call(
        functools.partial(_attn_prompt_body, tq=tq, tk=tk, n_heads=n_heads),
        out_shape=jax.ShapeDtypeStruct((batch, seq, n_heads * HEAD_DIM), BF16),
        grid=(batch, seq // tq),
        in_specs=[row(qw), row(LANES), per_b(kc), per_b(vc), per_b(ksa), per_b(vsa), per_b(kwb), per_b(vwa)],
        out_specs=row(n_heads * HEAD_DIM),
        scratch_shapes=[pltpu.VMEM((n_heads * tq, 2 * LANES), BF16),
                        pltpu.VMEM((n_heads * tq, LANES), F32),
                        pltpu.VMEM((n_heads * tq, 2 * LANES), F32),
                        pltpu.VMEM((2, n_heads * tq, tk), F32)],
        compiler_params=_params(("arbitrary", "arbitrary")),
        name="attn_prompt",
    )(q, gates, kc, vc, ksa, vsa, kwb, vwa)


_TP = SUBLANES
_KEY_CHUNKS = 4
_DMA_UNROLL = 4


def _attn_sample_body(pt_ref, q_ref, g_ref, new_ref, swk_ref, swv_ref, wck_ref, wcv_ref, perm_ref,
                      cck_hbm, ccv_hbm, csk_hbm, csv_hbm, o_ref,
                      bufs, sems, cs_ref, ka_ref, va_ref, kn_ref, vn_ref, kwn_ref, vwn_ref, vw_ref, qa_ref, s_ref,
                      *, n_heads, n_tok, n_pages, page):
    i = pl.program_id(0)
    n_seq = pl.num_programs(0)
    caches = (cck_hbm, ccv_hbm, csk_hbm, csv_hbm)
    hpg = n_heads // N_KV
    rows = n_heads * _TP
    past = n_pages * page
    wb = swk_ref.shape[1]

    def page_copy(a, seq, p):
        return pltpu.make_async_copy(caches[a].at[pt_ref[seq, p]], bufs.at[a, p], sems.at[a])

    def start(arrays, seq):
        def body(p, c):
            for a in arrays:
                page_copy(a, seq, p).start()
            return c
        lax.fori_loop(0, n_pages, body, 0, unroll=_DMA_UNROLL)

    def wait(a):
        def body(p, c):
            page_copy(a, 0, p).wait()
            return c
        lax.fori_loop(0, n_pages, body, 0, unroll=True)

    def cast_pages(a, dst):
        for p in range(n_pages):
            dst[0:KV_W, p * page:(p + 1) * page] = bufs[a, p].astype(BF16)

    @pl.when(i == 0)
    def _():
        start((0, 1), 0)
        start((2, 3), 0)
        blk = lax.broadcasted_iota(jnp.int32, (LANES, past), 0)
        kpos = lax.broadcasted_iota(jnp.int32, (LANES, past), 1)
        ka_ref[KV_W:, :] = jnp.where(kpos // L_SEL == blk, 1.0, 0.0).astype(BF16)
        vw_ref[KV_W:, :] = jnp.ones((LANES, wb), BF16)
        kn_ref[...] = jnp.zeros(kn_ref.shape, BF16)
        vn_ref[...] = jnp.zeros(vn_ref.shape, BF16)
        kwn_ref[...] = jnp.zeros(kwn_ref.shape, BF16)
        vwn_ref[...] = jnp.zeros(vwn_ref.shape, BF16)
        qa_ref[...] = jnp.zeros(qa_ref.shape, BF16)

    for h in range(n_heads):
        qa_ref[h * _TP:h * _TP + n_tok, 0:LANES] = q_ref[:, h * LANES:(h + 1) * LANES]
    qc = qa_ref[:, 0:LANES]
    tok = lax.broadcasted_iota(jnp.int32, (_TP, 1), 0)

    wait(0)
    wait(1)
    cast_pages(0, cs_ref.at[0])
    cast_pages(1, cs_ref.at[1])

    @pl.when(i + 1 < n_seq)
    def _():
        start((0, 1), i + 1)

    def summaries(a, w_ref):
        half = past // 2
        nat = jnp.concatenate(
            [jnp.dot(cs_ref[a, :, j * half:(j + 1) * half], w_ref[...], preferred_element_type=F32)
             for j in range(2)], axis=1).astype(BF16)
        return jnp.dot(nat, perm_ref[...], preferred_element_type=F32).astype(BF16)

    kct = summaries(0, wck_ref)
    vct = summaries(1, wcv_ref)
    o_cmp, imps = _cmp_attend(qc, kct, vct, None, n_heads, _TP, transposed=True)

    n_blk = past // L_SEL
    blk = lax.broadcasted_iota(jnp.int32, (_TP, LANES), 1)
    forced = (blk == 0) | (blk == n_blk - 1)
    v = jnp.concatenate([jnp.where(forced, jnp.inf, imp) for imp in imps]
                        + [jnp.full((LANES - N_KV * _TP, LANES), NEG, F32)], axis=0)
    bias = _select_bias(v, N_SEL - _N_FORCED)
    for h in range(n_heads):
        g = h // hpg
        qa_ref[h * _TP:(h + 1) * _TP, LANES:] = bias[g * _TP:(g + 1) * _TP].astype(BF16)

    wait(2)
    wait(3)
    cast_pages(2, ka_ref)
    cast_pages(3, va_ref)

    @pl.when(i + 1 < n_seq)
    def _():
        start((2, 3), i + 1)

    new = new_ref[...]
    kn_ref[0:n_tok, :] = new[:, 2 * KV_W:3 * KV_W].astype(BF16)
    vn_ref[0:n_tok, 0:LANES] = new[:, 3 * KV_W:4 * KV_W].astype(BF16)
    vn_ref[0:n_tok, LANES:] = jnp.ones((n_tok, LANES), BF16)
    jn = lax.broadcasted_iota(jnp.int32, (_TP, LANES), 1)
    new_ok = (jn <= tok) & (jn < n_tok)
    chunk = past // _KEY_CHUNKS
    for c in range(_KEY_CHUNKS):
        ks = slice(c * chunk, (c + 1) * chunk)
        s_ref[:, ks] = jnp.dot(qa_ref[...], ka_ref[:, ks], preferred_element_type=F32)
    s_new = lax.dot_general(qc, kn_ref[...], _NT, preferred_element_type=F32)
    s_new = _mask_heads(s_new, new_ok, _TP, NEG)
    m = _rowmax(s_new)
    for c in range(_KEY_CHUNKS):
        m = jnp.maximum(m, _rowmax(s_ref[:, c * chunk:(c + 1) * chunk]))
    acc = jnp.dot(jnp.exp(s_new - m).astype(BF16), vn_ref[...], preferred_element_type=F32)
    num, den = acc[:, 0:LANES], acc[:, LANES:]
    for c in range(_KEY_CHUNKS):
        ks = slice(c * chunk, (c + 1) * chunk)
        e = jnp.exp(s_ref[:, ks] - _rep(m, chunk // LANES))
        den = den + _rowsum(e)
        num = num + lax.dot_general(e.astype(BF16), va_ref[:, ks], _NT, preferred_element_type=F32)
    o_sel = num / den

    kwn_ref[0:n_tok, :] = new[:, 4 * KV_W:5 * KV_W].astype(BF16)
    vwn_ref[0:n_tok, 0:LANES] = new[:, 5 * KV_W:6 * KV_W].astype(BF16)
    vwn_ref[0:n_tok, LANES:] = jnp.ones((n_tok, LANES), BF16)
    vw_ref[0:KV_W, :] = swv_ref[...].astype(BF16)
    s_wp = jnp.dot(qc, swk_ref[...].astype(BF16), preferred_element_type=F32)
    s_wn = lax.dot_general(qc, kwn_ref[...], _NT, preferred_element_type=F32)
    c = lax.broadcasted_iota(jnp.int32, (_TP, wb), 1)
    s_wp = _mask_heads(s_wp, c - wb > tok - WINDOW, _TP, NEG)
    s_wn = _mask_heads(s_wn, new_ok, _TP, NEG)
    m = jnp.maximum(_rowmax(s_wp), _rowmax(s_wn))
    accw = (lax.dot_general(jnp.exp(s_wp - _rep(m, wb // LANES)).astype(BF16), vw_ref[...], _NT,
                            preferred_element_type=F32)
            + jnp.dot(jnp.exp(s_wn - m).astype(BF16), vwn_ref[...], preferred_element_type=F32))
    o_win = accw[:, 0:LANES] / accw[:, LANES:]

    gates = jnp.concatenate([g_ref[...], jnp.zeros((_TP - n_tok, LANES), F32)], axis=0)
    o_ref[...] = _gate_mix(gates, o_cmp, o_sel, o_win, _TP)[0:n_tok].astype(BF16)


def _attn_sample(page_table, q, gates, new, swk, swv, wck, wcv, perm, cck, ccv, csk, csv):
    n_seq, n_tok, qw = q.shape
    n_heads = qw // LANES
    n_pages = page_table.shape[1]
    page = cck.shape[2]
    past = n_pages * page
    wb = swk.shape[2]
    per_seq = lambda a: pl.BlockSpec((None,) + a.shape[1:], lambda i, pt: (i,) + (0,) * (a.ndim - 1))
    hbm = pl.BlockSpec(memory_space=pl.ANY)
    grid_spec = pltpu.PrefetchScalarGridSpec(
        num_scalar_prefetch=1,
        grid=(n_seq,),
        in_specs=[per_seq(q), per_seq(gates), per_seq(new), per_seq(swk), per_seq(swv),
                  _resident(wck.shape), _resident(wcv.shape), _resident(perm.shape), hbm, hbm, hbm, hbm],
        out_specs=pl.BlockSpec((None, n_tok, n_heads * HEAD_DIM), lambda i, pt: (i, 0, 0)),
        scratch_shapes=[pltpu.VMEM((4, n_pages, KV_W, page), F32),
                        pltpu.SemaphoreType.DMA((4,)),
                        pltpu.VMEM((2, KV_W, past), BF16),
                        pltpu.VMEM((2 * LANES, past), BF16),
                        pltpu.VMEM((KV_W, past), BF16),
                        pltpu.VMEM((LANES, KV_W), BF16),
                        pltpu.VMEM((LANES, 2 * LANES), BF16),
                        pltpu.VMEM((LANES, KV_W), BF16),
                        pltpu.VMEM((LANES, 2 * LANES), BF16),
                        pltpu.VMEM((2 * LANES, wb), BF16),
                        pltpu.VMEM((n_heads * _TP, 2 * LANES), BF16),
                        pltpu.VMEM((n_heads * _TP, past), F32)])
    return pl.pallas_call(
        functools.partial(_attn_sample_body, n_heads=n_heads, n_tok=n_tok, n_pages=n_pages, page=page),
        out_shape=jax.ShapeDtypeStruct((n_seq, n_tok, n_heads * HEAD_DIM), BF16),
        grid_spec=grid_spec,
        compiler_params=_params(("arbitrary",)),
        name="attn_sample",
    )(page_table, q, gates, new, swk, swv, wck, wcv, perm, cck, ccv, csk, csv)


def _cmp_placement(w, n_keys):
    k = jnp.arange(n_keys)
    hit = (k // L_CMP)[:, None] == jnp.arange(n_keys // L_CMP)[None, :]
    return jnp.where(hit, w[k % L_CMP][:, None], 0.0).astype(BF16)


def _even_odd_perm(nc):
    src = jnp.arange(nc)
    dst = src // 2 + (src % 2) * (nc // 2)
    return (dst[:, None] == jnp.arange(nc)[None, :]).astype(BF16)


def _out_ffn_body(on_ref, yp_ref, x_ref, g2_ref, sh_ref, sc_ref, gt_ref, gains_ref, wout_ref, wi_ref, wo_ref,
                  o_ref, act_ref, *, d_ff):
    half = on_ref.shape[1]
    y = (jnp.dot(on_ref[...], wout_ref[0:half, :], preferred_element_type=F32)
         + jnp.dot(yp_ref[...], wout_ref[half:, :], preferred_element_type=F32))
    x = x_ref[...] + g2_ref[...] * _rms(y, gains_ref[0:1, :])
    h = (_rms(x, gains_ref[1:2, :]) * (1 + sc_ref[...]) + sh_ref[...]).astype(BF16)
    y = _swiglu(h, wi_ref, wo_ref, act_ref, d_ff)
    o_ref[...] = x + 0.5 * gt_ref[...] * _rms(y, gains_ref[2:3, :])


def _out_ffn(o_nsa, y_pool, x, mods, gains, w_out, wi, wo, *, tm, per_row, tiles_per_seq):
    r, d = x.shape
    d_ff = wo.shape[0]
    ms = _mod_spec(per_row, tm, d, tiles_per_seq)
    row = lambda c: pl.BlockSpec((tm, c), lambda i: (i, 0))
    return pl.pallas_call(
        functools.partial(_out_ffn_body, d_ff=d_ff),
        out_shape=jax.ShapeDtypeStruct((r, d), F32),
        grid=(r // tm,),
        in_specs=[row(o_nsa.shape[1]), row(y_pool.shape[1]), row(d), ms, ms, ms, ms,
                  _resident((3, d)), _resident(w_out.shape), _resident(wi.shape), _resident(wo.shape)],
        out_specs=row(d),
        scratch_shapes=[pltpu.VMEM((tm, d_ff), BF16)],
        compiler_params=_params(("arbitrary",)),
        name="out_ffn",
    )(o_nsa, y_pool, x, *mods, gains, w_out, wi, wo)


def _even_odd(a):
    return jnp.concatenate([a[:, 0::2], a[:, 1::2]], axis=1).astype(BF16)


def kernel(x_prompt, x_sample, cache_cmp_k, cache_cmp_v, cache_sel_k, cache_sel_v, state_win_k, state_win_v,
           state_pool, page_table, c_prompt, c_sample, w_ada, b_ada, norm_gains, ffn1_wi, ffn1_wo, ffn2_wi,
           ffn2_wo, w_in, w_out, cmp_w, pool_w, pool_scale):
    depth = w_ada.shape[0]
    batch, seq, d = x_prompt.shape
    n_seq, n_tok, _ = x_sample.shape
    n_pages = page_table.shape[1]
    page = cache_cmp_k.shape[2]
    past = n_pages * page
    tm, tq, tk = 512, 128, 512
    assert depth == 1 and seq % tm == 0 and seq // L_SEL == LANES and past // L_SEL == LANES
    assert n_tok <= _TP and past % L_SEL == 0 and past >= max(POOL_WINDOWS) and seq >= WINDOW + tq
    assert state_pool.shape[2] == max(POOL_WINDOWS) - 1 and cache_cmp_k.shape[3:] == (N_KV, HEAD_DIM)
    l = 0
    n_rows_s = n_seq * n_tok
    tps = seq // tm

    mod = _modulation(jnp.concatenate([c_prompt, c_sample], axis=0), w_ada[l], b_ada[l])
    mod_p = [mod[:batch, k][:, None, :] for k in range(N_MOD)]
    mod_s = [jnp.tile(mod[batch:, k], (n_tok, 1)) for k in range(N_MOD)]
    gains = norm_gains[l]
    wi1, wo1 = ffn1_wi[l].astype(BF16), ffn1_wo[l].astype(BF16)
    wi2, wo2 = ffn2_wi[l].astype(BF16), ffn2_wo[l].astype(BF16)
    w_all = _prep_w_in(w_in[l])
    w_o = w_out[l].astype(BF16)
    pw = pool_w[l].astype(BF16)
    ps = pool_scale[l].reshape(1, -1)
    cmp_rows_p = jnp.tile(cmp_w[l][:, :, None], (1, tm // L_CMP, KV_W))

    xp = x_prompt.reshape(batch * seq, d)
    xs = jnp.swapaxes(x_sample, 0, 1).reshape(n_rows_s, d)

    xp = _ffn(xp, mod_p[0:3], gains[0:2], wi1, wo1, tm=tm, per_row=False, tiles_per_seq=tps)
    xs = _ffn(xs, mod_s[0:3], gains[0:2], wi1, wo1, tm=n_rows_s, per_row=True, tiles_per_seq=1)

    (kcr, vcr, ks, vs, kw, vw, kc, vc, qp, gp, ksa, vsa, kwb, vwa, yp_p, u_tail) = _inproj_prompt(
        xp, mod_p[3], mod_p[4], gains[2:3], w_all, cmp_rows_p, pw, ps, batch=batch, seq=seq, tm=tm)
    on_p = _attn_prompt(qp, gp, _even_odd(kc), _even_odd(vc), ksa, vsa, kwb, vwa, tq=tq, tk=tk)

    hist = jnp.swapaxes(state_pool[l], 0, 1)
    kv_s, q_s, g_s, yp_s, u_s = _inproj_sample(
        xs, mod_s[3], mod_s[4], gains[2:3], w_all, hist, pw, ps, n_seq=n_seq, n_tok=n_tok, past_len=past)
    by_seq = lambda a: jnp.swapaxes(a.reshape(n_tok, n_seq, a.shape[-1]), 0, 1)
    new_s = by_seq(kv_s)
    cols = lambda a: jnp.transpose(a, (0, 2, 3, 1)).reshape(a.shape[0], KV_W, a.shape[1])
    on_s = _attn_sample(page_table, by_seq(q_s), by_seq(g_s), new_s, cols(state_win_k[l]), cols(state_win_v[l]),
                        _cmp_placement(cmp_w[l, 0], past // 2), _cmp_placement(cmp_w[l, 1], past // 2),
                        _even_odd_perm(past // L_CMP),
                        cols(cache_cmp_k[l]), cols(cache_cmp_v[l]), cols(cache_sel_k[l]), cols(cache_sel_v[l]))
    on_s = jnp.swapaxes(on_s, 0, 1).reshape(n_rows_s, -1)

    xp = _out_ffn(on_p.reshape(batch * seq, -1), yp_p.reshape(batch * seq, -1), xp,
                  [mod_p[5], mod_p[6], mod_p[7], mod_p[8]], gains[3:6], w_o, wi2, wo2,
                  tm=tm, per_row=False, tiles_per_seq=tps)
    xs = _out_ffn(on_s, yp_s, xs, [mod_s[5], mod_s[6], mod_s[7], mod_s[8]], gains[3:6], w_o, wi2, wo2,
                  tm=n_rows_s, per_row=True, tiles_per_seq=1)

    heads = lambda a: a.reshape(a.shape[:-1] + (N_KV, HEAD_DIM))[None]
    keep = min(WINDOW, seq)
    n_state = state_pool.shape[2]
    y_prompt = xp.reshape(batch, seq, d)
    y_sample = jnp.swapaxes(xs.reshape(n_tok, n_seq, d), 0, 1)
    p_out = (heads(kcr), heads(vcr), heads(ks), heads(vs), heads(kw[:, seq - keep:]), heads(vw[:, seq - keep:]),
             u_tail[:, _HALO - n_state:][None])
    new4 = [heads(new_s[..., j * KV_W:(j + 1) * KV_W]) for j in range(6)]
    keep_s = min(WINDOW, state_win_k.shape[2] + n_tok)
    win_k = jnp.concatenate([state_win_k, new4[4]], axis=2)
    win_v = jnp.concatenate([state_win_v, new4[5]], axis=2)
    pool_s = jnp.concatenate([state_pool[l], by_seq(u_s)], axis=1)
    s_out = (new4[0], new4[1], new4[2], new4[3], win_k[:, :, win_k.shape[2] - keep_s:],
             win_v[:, :, win_v.shape[2] - keep_s:], pool_s[:, pool_s.shape[1] - n_state:][None])
    return (y_prompt, y_sample) + p_out + s_out
```

```python
import functools

import jax
import jax.numpy as jnp
from jax import lax
from jax.experimental import pallas as pl
from jax.experimental.pallas import tpu as pltpu

HEAD_DIM = 64
N_KV = 2
L_CMP = 32
L_SEL = 64
N_SEL = 16
WINDOW = 512
POOL_WINDOWS = (2, 4, 8, 16)
N_MOD = 9
RMS_EPS = 1e-6
NEG = -1e30

LANES = 128
SUBLANES = 8
VMEM_LIMIT_BYTES = 56 * 1024 * 1024

F32 = jnp.float32
BF16 = jnp.bfloat16

KV_W = N_KV * HEAD_DIM
_NT = (((1,), (1,)), ((), ()))


def _rms(x, g):
    return x * lax.rsqrt(jnp.mean(x * x, axis=-1, keepdims=True) + RMS_EPS) * g


def _silu(a):
    return a * jax.nn.sigmoid(a)


def _params(sem):
    return pltpu.CompilerParams(dimension_semantics=sem, vmem_limit_bytes=VMEM_LIMIT_BYTES)


def _resident(shape):
    return pl.BlockSpec(shape, lambda *_: (0,) * len(shape), pipeline_mode=pl.Buffered(1))


def _mod_body(c_ref, w_ref, b_ref, o_ref):
    a = _silu(c_ref[...]).astype(BF16)
    o_ref[...] = jnp.dot(a, w_ref[...].astype(BF16), preferred_element_type=F32) + b_ref[...]


def _modulation(c, w_ada, b_ada):
    n, d = c.shape
    n_pad = -(-n // SUBLANES) * SUBLANES
    c = jnp.pad(c, ((0, n_pad - n), (0, 0)))
    return pl.pallas_call(
        _mod_body,
        out_shape=jax.ShapeDtypeStruct((N_MOD, n_pad, d), F32),
        grid=(N_MOD,),
        in_specs=[pl.BlockSpec((n_pad, d), lambda j: (0, 0)),
                  pl.BlockSpec((d, d), lambda j: (0, j)),
                  pl.BlockSpec((1, d), lambda j: (0, j))],
        out_specs=pl.BlockSpec((None, n_pad, d), lambda j: (j, 0, 0)),
        compiler_params=_params(("arbitrary",)),
        name="adaln_modulation",
    )(c, w_ada, b_ada.reshape(1, -1))


def _ffn_chunks(d_ff):
    chunks, o = [], 0
    while o < d_ff:
        w = min(512, d_ff - o)
        chunks.append((o, w))
        o += w
    return chunks


def _swiglu(h, wi_ref, wo_ref, act_ref, d_ff):
    for o, w in _ffn_chunks(d_ff):
        a = jnp.dot(h, wi_ref[:, o:o + w], preferred_element_type=F32)
        b = jnp.dot(h, wi_ref[:, d_ff + o:d_ff + o + w], preferred_element_type=F32)
        act_ref[:, o:o + w] = (_silu(a) * b).astype(BF16)
    return jnp.dot(act_ref[...], wo_ref[...], preferred_element_type=F32)


def _ffn_body(x_ref, sh_ref, sc_ref, gt_ref, gains_ref, wi_ref, wo_ref, o_ref, act_ref, *, d_ff):
    x = x_ref[...]
    h = (_rms(x, gains_ref[0:1, :]) * (1 + sc_ref[...]) + sh_ref[...]).astype(BF16)
    y = _swiglu(h, wi_ref, wo_ref, act_ref, d_ff)
    o_ref[...] = x + 0.5 * gt_ref[...] * _rms(y, gains_ref[1:2, :])


def _mod_spec(k, per_row, tm, d, tiles_per_seq):
    if per_row:
        return pl.BlockSpec((None, tm, d), lambda i: (k, i, 0))
    return pl.BlockSpec((None, None, 1, d), lambda i: (k, i // tiles_per_seq, 0, 0))


def _ffn(x, mod, ks, gains, wi, wo, *, tm, per_row, tiles_per_seq):
    r, d = x.shape
    d_ff = wo.shape[0]
    return pl.pallas_call(
        functools.partial(_ffn_body, d_ff=d_ff),
        out_shape=jax.ShapeDtypeStruct((r, d), F32),
        grid=(r // tm,),
        in_specs=[pl.BlockSpec((tm, d), lambda i: (i, 0))]
                 + [_mod_spec(k, per_row, tm, d, tiles_per_seq) for k in ks]
                 + [_resident((2, d)), _resident(wi.shape), _resident(wo.shape)],
        out_specs=pl.BlockSpec((tm, d), lambda i: (i, 0)),
        scratch_shapes=[pltpu.VMEM((tm, d_ff), BF16)],
        compiler_params=_params(("arbitrary",)),
        name="ffn",
    )(x, *([mod] * len(ks)), gains, wi, wo)


_QW = 8 * LANES
_O_KC, _O_VC, _O_KS, _O_VS, _O_KW, _O_VW = (_QW + i * KV_W for i in range(6))
_O_G = _QW + 6 * KV_W
_O_U = _O_G + LANES
_POOL_W = len(POOL_WINDOWS) * LANES
_IN_COLS = _O_U + _POOL_W
_HALO = 16


def _prep_w_in(w_in):
    d = w_in.shape[0]
    n_heads = (_QW // LANES)
    hpg = n_heads // N_KV
    q = w_in[:, :n_heads * HEAD_DIM].reshape(d, n_heads, 1, HEAD_DIM)
    own_group = (jnp.arange(n_heads) // hpg)[:, None] == jnp.arange(N_KV)[None, :]
    slot = jnp.where(own_group[None, :, :, None], q, 0.0)
    o = n_heads * HEAD_DIM
    kv = w_in[:, o:o + 6 * KV_W]
    o += 6 * KV_W
    n_g = n_heads * 3
    g = jnp.pad(w_in[:, o:o + n_g], ((0, 0), (0, LANES - n_g)))
    u = w_in[:, o + n_g:]
    return jnp.concatenate([slot.reshape(d, _QW), kv, g, u], axis=1).astype(BF16)


def _project(x_ref, sh_ref, sc_ref, gain_ref, w_ref):
    h = (_rms(x_ref[...], gain_ref[...]) * (1 + sc_ref[...]) + sh_ref[...]).astype(BF16)
    return jnp.dot(h, w_ref[...], preferred_element_type=F32)


def _pool_out(d, pw_ref, ps_ref):
    parts = [jnp.dot(d[:, g * LANES:(g + 1) * LANES].astype(BF16), pw_ref[g], preferred_element_type=F32)
             for g in range(len(POOL_WINDOWS))]
    return (jnp.concatenate(parts, axis=1) * ps_ref[...]).astype(BF16)


def _inproj_prompt_body(x_ref, sh_ref, sc_ref, gain_ref, w_ref, cw_ref, pw_ref, ps_ref,
                        kcr_ref, vcr_ref, ks_ref, vs_ref, kw_ref, vw_ref, kc_ref, vc_ref,
                        q_ref, g_ref, ksa_ref, vsa_ref, kwb_ref, vwa_ref, yp_ref, ul_ref,
                        ext_ref, *, tm):
    i = pl.program_id(1)
    z = _project(x_ref, sh_ref, sc_ref, gain_ref, w_ref)
    kcr, vcr = z[:, _O_KC:_O_KC + KV_W], z[:, _O_VC:_O_VC + KV_W]
    ks, vs = z[:, _O_KS:_O_KS + KV_W], z[:, _O_VS:_O_VS + KV_W]
    kw, vw = z[:, _O_KW:_O_KW + KV_W], z[:, _O_VW:_O_VW + KV_W]
    kcr_ref[...], vcr_ref[...] = kcr, vcr
    ks_ref[...], vs_ref[...] = ks, vs
    kw_ref[...], vw_ref[...] = kw, vw
    kc_ref[...] = (kcr * cw_ref[0]).reshape(tm // L_CMP, L_CMP, KV_W).sum(axis=1)
    vc_ref[...] = (vcr * cw_ref[1]).reshape(tm // L_CMP, L_CMP, KV_W).sum(axis=1)
    q_ref[...] = (z[:, :_QW] * (HEAD_DIM ** -0.5)).astype(BF16)
    g_ref[...] = jax.nn.sigmoid(z[:, _O_G:_O_G + LANES])
    pos = i * tm + lax.broadcasted_iota(jnp.int32, (tm, LANES), 0)
    blk = lax.broadcasted_iota(jnp.int32, (tm, LANES), 1)
    onehot = jnp.where(pos // L_SEL == blk, 1.0, 0.0).astype(BF16)
    ones = jnp.ones((tm, LANES), BF16)
    ksa_ref[...] = jnp.concatenate([ks.astype(BF16), onehot], axis=1)
    vsa_ref[...] = jnp.concatenate([vs.astype(BF16), ones], axis=1)
    kwb_ref[...] = kw.astype(BF16)
    vwa_ref[...] = jnp.concatenate([vw.astype(BF16), ones], axis=1)
    u = z[:, _O_U:_O_U + _POOL_W]

    @pl.when(i == 0)
    def _():
        ext_ref[0:_HALO, :] = jnp.zeros((_HALO, _POOL_W), F32)

    ext_ref[_HALO:, :] = u
    tpos = i * tm + lax.broadcasted_iota(jnp.int32, (tm, LANES), 0)
    diffs = []
    for g, w in enumerate(POOL_WINDOWS):
        s = ext_ref[:, g * LANES:(g + 1) * LANES]
        k = 1
        while k < w:
            s = s + pltpu.roll(s, k, 0)
            k *= 2
        cnt = jnp.minimum(w, tpos + 1).astype(F32)
        diffs.append(s[_HALO:] / cnt - u[:, g * LANES:(g + 1) * LANES])
    yp_ref[...] = _pool_out(jnp.concatenate(diffs, axis=1), pw_ref, ps_ref)
    tail = u[tm - _HALO:, :]
    ext_ref[0:_HALO, :] = tail
    ul_ref[...] = tail


def _inproj_prompt(x, mod, ks, gain, w_all, cmp_rows, pool_w, pool_scale, *, batch, seq, tm):
    d = x.shape[1]
    tps = seq // tm
    row = lambda c: pl.BlockSpec((None, tm, c), lambda b, i: (b, i, 0))
    mods = [pl.BlockSpec((None, None, 1, d), lambda b, i, k=k: (k, b, 0, 0)) for k in ks]
    f32o = lambda c: jax.ShapeDtypeStruct((batch, seq, c), F32)
    bf16o = lambda c: jax.ShapeDtypeStruct((batch, seq, c), BF16)
    out_shape = ([f32o(KV_W)] * 6
                 + [jax.ShapeDtypeStruct((batch, seq // L_CMP, KV_W), F32)] * 2
                 + [bf16o(_QW), f32o(LANES), bf16o(2 * LANES), bf16o(2 * LANES), bf16o(KV_W), bf16o(2 * LANES),
                    bf16o(_POOL_W), jax.ShapeDtypeStruct((batch, _HALO, _POOL_W), F32)])
    out_specs = ([row(KV_W)] * 6
                 + [pl.BlockSpec((None, tm // L_CMP, KV_W), lambda b, i: (b, i, 0))] * 2
                 + [row(_QW), row(LANES), row(2 * LANES), row(2 * LANES), row(KV_W), row(2 * LANES),
                    row(_POOL_W), pl.BlockSpec((None, _HALO, _POOL_W), lambda b, i: (b, 0, 0))])
    return pl.pallas_call(
        functools.partial(_inproj_prompt_body, tm=tm),
        out_shape=out_shape,
        grid=(batch, tps),
        in_specs=[pl.BlockSpec((tm, d), lambda b, i: (b * tps + i, 0))] + mods
                 + [_resident((1, d)), _resident(w_all.shape), _resident(cmp_rows.shape),
                    _resident(pool_w.shape), _resident(pool_scale.shape)],
        out_specs=out_specs,
        scratch_shapes=[pltpu.VMEM((_HALO + tm, _POOL_W), F32)],
        compiler_params=_params(("arbitrary", "arbitrary")),
        name="inproj_prompt",
    )(x, *([mod] * len(ks)), gain, w_all, cmp_rows, pool_w, pool_scale)


def _inproj_sample_body(x_ref, sh_ref, sc_ref, gain_ref, w_ref, hist_ref, pw_ref, ps_ref,
                        kv_ref, q_ref, g_ref, yp_ref, u_ref, *, n_seq, n_tok, past_len):
    z = _project(x_ref, sh_ref, sc_ref, gain_ref, w_ref)
    kv_ref[...] = z[:, _O_KC:_O_KC + 6 * KV_W]
    q_ref[...] = (z[:, :_QW] * (HEAD_DIM ** -0.5)).astype(BF16)
    g_ref[...] = jax.nn.sigmoid(z[:, _O_G:_O_G + LANES])
    u = z[:, _O_U:_O_U + _POOL_W]
    u_ref[...] = u
    n_hist = hist_ref.shape[0]
    slabs = [hist_ref[j] for j in range(n_hist)] + [u[t * n_seq:(t + 1) * n_seq] for t in range(n_tok)]
    rows = []
    for t in range(n_tok):
        diffs = []
        for g, w in enumerate(POOL_WINDOWS):
            lanes = slice(g * LANES, (g + 1) * LANES)
            s = slabs[n_hist + t][:, lanes]
            for j in range(1, w):
                s = s + slabs[n_hist + t - j][:, lanes]
            diffs.append(s / float(min(w, past_len + t + 1)) - slabs[n_hist + t][:, lanes])
        rows.append(jnp.concatenate(diffs, axis=1))
    yp_ref[...] = _pool_out(jnp.concatenate(rows, axis=0), pw_ref, ps_ref)


def _inproj_sample(x, mod, ks, gain, w_all, hist, pool_w, pool_scale, *, n_seq, n_tok, past_len):
    r, d = x.shape
    full = lambda a: pl.BlockSpec(a.shape, lambda i: (0,) * a.ndim)
    out_shape = [jax.ShapeDtypeStruct((r, 6 * KV_W), F32), jax.ShapeDtypeStruct((r, _QW), BF16),
                 jax.ShapeDtypeStruct((r, LANES), F32), jax.ShapeDtypeStruct((r, _POOL_W), BF16),
                 jax.ShapeDtypeStruct((r, _POOL_W), F32)]
    rest = (gain, w_all, hist, pool_w, pool_scale)
    return pl.pallas_call(
        functools.partial(_inproj_sample_body, n_seq=n_seq, n_tok=n_tok, past_len=past_len),
        out_shape=out_shape,
        grid=(1,),
        in_specs=[full(x)] + [_mod_spec(k, True, r, d, 1) for k in ks] + [full(a) for a in rest],
        out_specs=[pl.BlockSpec(s.shape, lambda i: (0, 0)) for s in out_shape],
        compiler_params=_params(("arbitrary",)),
        name="inproj_sample",
    )(x, *([mod] * len(ks)), *rest)


_N_FORCED = 3


def _select_bias(v, n_pick):
    vt = v.T
    blk = lax.broadcasted_iota(jnp.int32, vt.shape, 0).astype(F32)
    always = vt == jnp.inf
    bias = jnp.where(always, 0.0, NEG)
    vt = jnp.where(always, -jnp.inf, vt)
    for _ in range(n_pick):
        m = jnp.max(vt, axis=0, keepdims=True)
        first = jnp.min(jnp.where(vt == m, blk, float(LANES)), axis=0, keepdims=True)
        hit = blk == first
        bias = jnp.where(hit, 0.0, bias)
        vt = jnp.where(hit, -jnp.inf, vt)
    return bias.T


def _rep(m, n):
    return jnp.concatenate([m] * n, axis=1) if n > 1 else m


def _rowmax(s):
    return jnp.broadcast_to(jnp.max(s, axis=-1, keepdims=True), (s.shape[0], LANES))


def _rowsum(s):
    return jnp.broadcast_to(jnp.sum(s, axis=-1, keepdims=True), (s.shape[0], LANES))


def _mask_heads(s, mask, rows, fill):
    n, c = s.shape[0] // rows, s.shape[1]
    return jnp.where(mask[None], s.reshape(n, rows, c), fill).reshape(n * rows, c)


def _gate_mix(gates, o_cmp, o_sel, o_win, rows):
    n_heads = o_cmp.shape[0] // rows
    hpg = n_heads // N_KV
    parts = []
    for h in range(n_heads):
        g = h // hpg
        lanes = slice(g * HEAD_DIM, (g + 1) * HEAD_DIM)
        rs = slice(h * rows, (h + 1) * rows)
        parts.append(gates[:, 3 * h:3 * h + 1] * o_cmp[rs, lanes]
                     + gates[:, 3 * h + 1:3 * h + 2] * o_sel[rs, lanes]
                     + gates[:, 3 * h + 2:3 * h + 3] * o_win[rs, lanes])
    return jnp.concatenate(parts, axis=1)


def _cmp_attend(qc, kc, vc, valid, n_heads, rows, transposed=False):
    nc = kc.shape[1] if transposed else kc.shape[0]
    if transposed:
        s = jnp.dot(qc, kc, preferred_element_type=F32)
    else:
        s = lax.dot_general(qc, kc, _NT, preferred_element_type=F32)
    if valid is not None:
        s = _mask_heads(s, valid, rows, NEG)
    e = jnp.exp(s - _rep(_rowmax(s), nc // LANES))
    p = e / _rep(_rowsum(e), nc // LANES)
    if valid is not None:
        p = _mask_heads(p, valid, rows, 0.0)
    pb = p.astype(BF16)
    if transposed:
        o = lax.dot_general(pb, vc, _NT, preferred_element_type=F32)
    else:
        o = jnp.dot(pb, vc, preferred_element_type=F32)
    hpg = n_heads // N_KV
    imps = []
    for g in range(N_KV):
        ph = p[g * hpg * rows:(g * hpg + 1) * rows]
        for h in range(1, hpg):
            ph = ph + p[(g * hpg + h) * rows:(g * hpg + h + 1) * rows]
        imps.append(ph[:, :nc // 2] + ph[:, nc // 2:])
    return o, imps


_ROW_SPLIT = 2


def _attn_prompt_body(q_ref, g_ref, kc_ref, vc_ref, ksa_ref, vsa_ref, kwb_ref, vwa_ref, o_ref,
                      qa_ref, m_ref, acc_ref, s_ref, *, tq, tk, n_heads):
    t0 = pl.program_id(1) * tq
    rows = n_heads * tq
    hpg = n_heads // N_KV
    hr = rows // _ROW_SPLIT
    for h in range(n_heads):
        qa_ref[h * tq:(h + 1) * tq, 0:LANES] = q_ref[:, h * LANES:(h + 1) * LANES]
    qc = qa_ref[:, 0:LANES]
    qpos = t0 + lax.broadcasted_iota(jnp.int32, (tq, 1), 0)

    nc = kc_ref.shape[0]
    col = lax.broadcasted_iota(jnp.int32, (tq, nc), 1)
    cblk = jnp.where(col < nc // 2, 2 * col, 2 * (col - nc // 2) + 1)
    valid = (cblk + 1) * L_CMP - 1 <= qpos
    o_cmp, imps = _cmp_attend(qc, kc_ref[...], vc_ref[...], valid, n_heads, tq)

    blk = lax.broadcasted_iota(jnp.int32, (tq, LANES), 1)
    cur = qpos // L_SEL
    forced = (blk == 0) | (blk == cur) | (blk == cur - 1)
    started = blk * L_SEL <= qpos
    v = [jnp.where(forced, jnp.inf, jnp.where(started, imp, NEG)) for imp in imps]
    bias = _select_bias(jnp.concatenate(v, axis=0), N_SEL - _N_FORCED)
    for h in range(n_heads):
        g = h // hpg
        qa_ref[h * tq:(h + 1) * tq, LANES:2 * LANES] = bias[g * tq:(g + 1) * tq].astype(BF16)

    span = WINDOW + tq
    w0 = pl.multiple_of(jnp.maximum(t0 - WINDOW, 0), tq)
    kposw = w0 + lax.broadcasted_iota(jnp.int32, (tq, span), 1)
    band = (kposw <= qpos) & (kposw > qpos - WINDOW)
    o_wins = []
    for r in range(_ROW_SPLIT):
        s = lax.dot_general(qa_ref[r * hr:(r + 1) * hr, 0:LANES], kwb_ref[pl.ds(w0, span), :], _NT,
                            preferred_element_type=F32)
        s = _mask_heads(s, band, tq, NEG)
        p = jnp.exp(s - _rep(_rowmax(s), span // LANES)).astype(BF16)
        accw = jnp.dot(p, vwa_ref[pl.ds(w0, span), :], preferred_element_type=F32)
        o_wins.append(accw[:, 0:LANES] / accw[:, LANES:2 * LANES])
    o_win = jnp.concatenate(o_wins, axis=0)

    m_ref[...] = jnp.full(m_ref.shape, -jnp.inf, F32)
    acc_ref[...] = jnp.zeros(acc_ref.shape, F32)

    def scores(kt, slot):
        ks0 = pl.multiple_of(kt * tk, tk)
        s_ref[slot] = lax.dot_general(qa_ref[...], ksa_ref[pl.ds(ks0, tk), :], _NT, preferred_element_type=F32)

    def accumulate(kt, slot, causal):
        ks0 = pl.multiple_of(kt * tk, tk)
        s = s_ref[slot]
        if causal:
            kpos = ks0 + lax.broadcasted_iota(jnp.int32, (tq, tk), 1)
            s = _mask_heads(s, kpos <= qpos, tq, NEG)
        m_prev = m_ref[...]
        m_new = jnp.maximum(m_prev, jnp.max(s, axis=-1, keepdims=True))
        alpha = jnp.exp(m_prev - m_new)
        p = jnp.exp(s - _rep(m_new, tk // LANES)).astype(BF16)
        pv = jnp.dot(p, vsa_ref[pl.ds(ks0, tk), :], preferred_element_type=F32)
        acc_ref[...] = acc_ref[...] * _rep(alpha, 2) + pv
        m_ref[...] = m_new

    n_full = t0 // tk
    scores(0, 0)

    def tile_pair(jj, carry):
        j = 2 * jj
        scores(j + 1, 1)
        accumulate(j, 0, False)
        scores(j + 2, 0)
        accumulate(j + 1, 1, False)
        return carry

    lax.fori_loop(0, n_full // 2, tile_pair, 0)

    @pl.when(n_full % 2 == 1)
    def _():
        scores(n_full, 1)
        accumulate(n_full - 1, 0, False)
        accumulate(n_full, 1, True)

    @pl.when(n_full % 2 == 0)
    def _():
        accumulate(n_full, 0, True)
    acc = acc_ref[...]
    o_sel = acc[:, 0:LANES] / acc[:, LANES:2 * LANES]

    o_ref[...] = _gate_mix(g_ref[...], o_cmp, o_sel, o_win, tq).astype(BF16)


def _attn_prompt(q, gates, kc, vc, ksa, vsa, kwb, vwa, *, tq, tk):
    batch, seq, qw = q.shape
    n_heads = qw // LANES
    per_b = lambda a: pl.BlockSpec((None,) + a.shape[1:], lambda b, i: (b,) + (0,) * (a.ndim - 1))
    row = lambda c: pl.BlockSpec((None, tq, c), lambda b, i: (b, i, 0))
    return pl.pallas_call(
        functools.partial(_attn_prompt_body, tq=tq, tk=tk, n_heads=n_heads),
        out_shape=jax.ShapeDtypeStruct((batch, seq, n_heads * HEAD_DIM), BF16),
        grid=(batch, seq // tq),
        in_specs=[row(qw), row(LANES), per_b(kc), per_b(vc), per_b(ksa), per_b(vsa), per_b(kwb), per_b(vwa)],
        out_specs=row(n_heads * HEAD_DIM),
        scratch_shapes=[pltpu.VMEM((n_heads * tq, 2 * LANES), BF16),
                        pltpu.VMEM((n_heads * tq, LANES), F32),
                        pltpu.VMEM((n_heads * tq, 2 * LANES), F32),
                        pltpu.VMEM((2, n_heads * tq, tk), F32)],
        compiler_params=_params(("arbitrary", "arbitrary")),
        name="attn_prompt",
    )(q, gates, kc, vc, ksa, vsa, kwb, vwa)


_TP = SUBLANES
_KEY_CHUNKS = 4
_DMA_UNROLL = 4


def _attn_sample_body(pt_ref, q_ref, g_ref, new_ref, swk_ref, swv_ref, wck_ref, wcv_ref, perm_ref,
                      cck_hbm, ccv_hbm, csk_hbm, csv_hbm, o_ref,
                      bufs, sems, cs_ref, ka_ref, va_ref, kn_ref, vn_ref, kwn_ref, vwn_ref, vw_ref, qa_ref, s_ref,
                      *, n_heads, n_tok, n_pages, page):
    i = pl.program_id(0)
    n_seq = pl.num_programs(0)
    caches = (cck_hbm, ccv_hbm, csk_hbm, csv_hbm)
    hpg = n_heads // N_KV
    rows = n_heads * _TP
    past = n_pages * page
    wb = swk_ref.shape[1]

    def page_copy(a, seq, p):
        return pltpu.make_async_copy(caches[a].at[pt_ref[seq, p]], bufs.at[a, p], sems.at[a])

    def start(arrays, seq):
        def body(p, c):
            for a in arrays:
                page_copy(a, seq, p).start(priority=a % 2)
            return c
        lax.fori_loop(0, n_pages, body, 0, unroll=_DMA_UNROLL)

    def wait(a):
        def body(p, c):
            page_copy(a, 0, p).wait()
            return c
        lax.fori_loop(0, n_pages, body, 0, unroll=True)

    def cast_pages(a, dst):
        for p in range(n_pages):
            dst[0:KV_W, p * page:(p + 1) * page] = bufs[a, p].astype(BF16)

    @pl.when(i == 0)
    def _():
        start((0, 1), 0)
        start((2, 3), 0)
        blk = lax.broadcasted_iota(jnp.int32, (LANES, past), 0)
        kpos = lax.broadcasted_iota(jnp.int32, (LANES, past), 1)
        ka_ref[KV_W:, :] = jnp.where(kpos // L_SEL == blk, 1.0, 0.0).astype(BF16)
        vw_ref[KV_W:, :] = jnp.ones((LANES, wb), BF16)
        kn_ref[...] = jnp.zeros(kn_ref.shape, BF16)
        vn_ref[...] = jnp.zeros(vn_ref.shape, BF16)
        kwn_ref[...] = jnp.zeros(kwn_ref.shape, BF16)
        vwn_ref[...] = jnp.zeros(vwn_ref.shape, BF16)
        qa_ref[...] = jnp.zeros(qa_ref.shape, BF16)

    for h in range(n_heads):
        qa_ref[h * _TP:h * _TP + n_tok, 0:LANES] = q_ref[:, h * LANES:(h + 1) * LANES]
    qc = qa_ref[:, 0:LANES]
    tok = lax.broadcasted_iota(jnp.int32, (_TP, 1), 0)

    wait(0)
    wait(1)
    cast_pages(0, cs_ref.at[0])
    cast_pages(1, cs_ref.at[1])

    @pl.when(i + 1 < n_seq)
    def _():
        start((0, 1), i + 1)

    def summaries(a, w_ref):
        half = past // 2
        nat = jnp.concatenate(
            [jnp.dot(cs_ref[a, :, j * half:(j + 1) * half], w_ref[...], preferred_element_type=F32)
             for j in range(2)], axis=1).astype(BF16)
        return jnp.dot(nat, perm_ref[...], preferred_element_type=F32).astype(BF16)

    kct = summaries(0, wck_ref)
    vct = summaries(1, wcv_ref)
    o_cmp, imps = _cmp_attend(qc, kct, vct, None, n_heads, _TP, transposed=True)

    n_blk = past // L_SEL
    blk = lax.broadcasted_iota(jnp.int32, (_TP, LANES), 1)
    forced = (blk == 0) | (blk == n_blk - 1)
    v = jnp.concatenate([jnp.where(forced, jnp.inf, imp) for imp in imps]
                        + [jnp.full((LANES - N_KV * _TP, LANES), NEG, F32)], axis=0)
    bias = _select_bias(v, N_SEL - _N_FORCED)
    for h in range(n_heads):
        g = h // hpg
        qa_ref[h * _TP:(h + 1) * _TP, LANES:] = bias[g * _TP:(g + 1) * _TP].astype(BF16)

    wait(2)
    wait(3)
    cast_pages(2, ka_ref)
    cast_pages(3, va_ref)

    @pl.when(i + 1 < n_seq)
    def _():
        start((2, 3), i + 1)

    new = new_ref[...]
    kn_ref[0:n_tok, :] = new[:, 2 * KV_W:3 * KV_W].astype(BF16)
    vn_ref[0:n_tok, 0:LANES] = new[:, 3 * KV_W:4 * KV_W].astype(BF16)
    vn_ref[0:n_tok, LANES:] = jnp.ones((n_tok, LANES), BF16)
    jn = lax.broadcasted_iota(jnp.int32, (_TP, LANES), 1)
    new_ok = (jn <= tok) & (jn < n_tok)
    chunk = past // _KEY_CHUNKS
    for c in range(_KEY_CHUNKS):
        ks = slice(c * chunk, (c + 1) * chunk)
        s_ref[:, ks] = jnp.dot(qa_ref[...], ka_ref[:, ks], preferred_element_type=F32)
    s_new = lax.dot_general(qc, kn_ref[...], _NT, preferred_element_type=F32)
    s_new = _mask_heads(s_new, new_ok, _TP, NEG)
    m = _rowmax(s_new)
    for c in range(_KEY_CHUNKS):
        m = jnp.maximum(m, _rowmax(s_ref[:, c * chunk:(c + 1) * chunk]))
    acc = jnp.dot(jnp.exp(s_new - m).astype(BF16), vn_ref[...], preferred_element_type=F32)
    num, den = acc[:, 0:LANES], acc[:, LANES:]
    for c in range(_KEY_CHUNKS):
        ks = slice(c * chunk, (c + 1) * chunk)
        e = jnp.exp(s_ref[:, ks] - _rep(m, chunk // LANES))
        den = den + _rowsum(e)
        num = num + lax.dot_general(e.astype(BF16), va_ref[:, ks], _NT, preferred_element_type=F32)
    o_sel = num / den

    kwn_ref[0:n_tok, :] = new[:, 4 * KV_W:5 * KV_W].astype(BF16)
    vwn_ref[0:n_tok, 0:LANES] = new[:, 5 * KV_W:6 * KV_W].astype(BF16)
    vwn_ref[0:n_tok, LANES:] = jnp.ones((n_tok, LANES), BF16)
    vw_ref[0:KV_W, :] = swv_ref[...].astype(BF16)
    s_wp = jnp.dot(qc, swk_ref[...].astype(BF16), preferred_element_type=F32)
    s_wn = lax.dot_general(qc, kwn_ref[...], _NT, preferred_element_type=F32)
    c = lax.broadcasted_iota(jnp.int32, (_TP, wb), 1)
    s_wp = _mask_heads(s_wp, c - wb > tok - WINDOW, _TP, NEG)
    s_wn = _mask_heads(s_wn, new_ok, _TP, NEG)
    m = jnp.maximum(_rowmax(s_wp), _rowmax(s_wn))
    accw = (lax.dot_general(jnp.exp(s_wp - _rep(m, wb // LANES)).astype(BF16), vw_ref[...], _NT,
                            preferred_element_type=F32)
            + jnp.dot(jnp.exp(s_wn - m).astype(BF16), vwn_ref[...], preferred_element_type=F32))
    o_win = accw[:, 0:LANES] / accw[:, LANES:]

    gates = jnp.concatenate([g_ref[...], jnp.zeros((_TP - n_tok, LANES), F32)], axis=0)
    o_ref[...] = _gate_mix(gates, o_cmp, o_sel, o_win, _TP)[0:n_tok].astype(BF16)


def _attn_sample(page_table, q, gates, new, swk, swv, wck, wcv, perm, cck, ccv, csk, csv):
    n_seq, n_tok, qw = q.shape
    n_heads = qw // LANES
    n_pages = page_table.shape[1]
    page = cck.shape[2]
    past = n_pages * page
    wb = swk.shape[2]
    per_seq = lambda a: pl.BlockSpec((None,) + a.shape[1:], lambda i, pt: (i,) + (0,) * (a.ndim - 1))
    hbm = pl.BlockSpec(memory_space=pl.ANY)
    grid_spec = pltpu.PrefetchScalarGridSpec(
        num_scalar_prefetch=1,
        grid=(n_seq,),
        in_specs=[per_seq(q), per_seq(gates), per_seq(new), per_seq(swk), per_seq(swv),
                  _resident(wck.shape), _resident(wcv.shape), _resident(perm.shape), hbm, hbm, hbm, hbm],
        out_specs=pl.BlockSpec((None, n_tok, n_heads * HEAD_DIM), lambda i, pt: (i, 0, 0)),
        scratch_shapes=[pltpu.VMEM((4, n_pages, KV_W, page), F32),
                        pltpu.SemaphoreType.DMA((4,)),
                        pltpu.VMEM((2, KV_W, past), BF16),
                        pltpu.VMEM((2 * LANES, past), BF16),
                        pltpu.VMEM((KV_W, past), BF16),
                        pltpu.VMEM((LANES, KV_W), BF16),
                        pltpu.VMEM((LANES, 2 * LANES), BF16),
                        pltpu.VMEM((LANES, KV_W), BF16),
                        pltpu.VMEM((LANES, 2 * LANES), BF16),
                        pltpu.VMEM((2 * LANES, wb), BF16),
                        pltpu.VMEM((n_heads * _TP, 2 * LANES), BF16),
                        pltpu.VMEM((n_heads * _TP, past), F32)])
    return pl.pallas_call(
        functools.partial(_attn_sample_body, n_heads=n_heads, n_tok=n_tok, n_pages=n_pages, page=page),
        out_shape=jax.ShapeDtypeStruct((n_seq, n_tok, n_heads * HEAD_DIM), BF16),
        grid_spec=grid_spec,
        compiler_params=_params(("arbitrary",)),
        name="attn_sample",
    )(page_table, q, gates, new, swk, swv, wck, wcv, perm, cck, ccv, csk, csv)


def _cmp_placement(w, n_keys):
    n_blk = n_keys // L_CMP
    hit = (jnp.arange(n_keys) // L_CMP)[:, None] == jnp.arange(n_blk)[None, :]
    return jnp.where(hit, jnp.tile(w, n_blk)[:, None], 0.0).astype(BF16)


def _even_odd_perm(nc):
    src = jnp.arange(nc)
    dst = src // 2 + (src % 2) * (nc // 2)
    return (dst[:, None] == jnp.arange(nc)[None, :]).astype(BF16)


def _out_ffn_body(on_ref, yp_ref, x_ref, g2_ref, sh_ref, sc_ref, gt_ref, gains_ref, wout_ref, wi_ref, wo_ref,
                  o_ref, act_ref, *, d_ff):
    half = on_ref.shape[1]
    y = (jnp.dot(on_ref[...], wout_ref[0:half, :], preferred_element_type=F32)
         + jnp.dot(yp_ref[...], wout_ref[half:, :], preferred_element_type=F32))
    x = x_ref[...] + g2_ref[...] * _rms(y, gains_ref[0:1, :])
    h = (_rms(x, gains_ref[1:2, :]) * (1 + sc_ref[...]) + sh_ref[...]).astype(BF16)
    y = _swiglu(h, wi_ref, wo_ref, act_ref, d_ff)
    o_ref[...] = x + 0.5 * gt_ref[...] * _rms(y, gains_ref[2:3, :])


def _out_ffn(o_nsa, y_pool, x, mod, ks, gains, w_out, wi, wo, *, tm, per_row, tiles_per_seq):
    r, d = x.shape
    d_ff = wo.shape[0]
    row = lambda c: pl.BlockSpec((tm, c), lambda i: (i, 0))
    return pl.pallas_call(
        functools.partial(_out_ffn_body, d_ff=d_ff),
        out_shape=jax.ShapeDtypeStruct((r, d), F32),
        grid=(r // tm,),
        in_specs=[row(o_nsa.shape[1]), row(y_pool.shape[1]), row(d)]
                 + [_mod_spec(k, per_row, tm, d, tiles_per_seq) for k in ks]
                 + [_resident((3, d)), _resident(w_out.shape), _resident(wi.shape), _resident(wo.shape)],
        out_specs=row(d),
        scratch_shapes=[pltpu.VMEM((tm, d_ff), BF16)],
        compiler_params=_params(("arbitrary",)),
        name="out_ffn",
    )(o_nsa, y_pool, x, *([mod] * len(ks)), gains, w_out, wi, wo)


def _even_odd(a):
    return jnp.concatenate([a[:, 0::2], a[:, 1::2]], axis=1).astype(BF16)


def kernel(x_prompt, x_sample, cache_cmp_k, cache_cmp_v, cache_sel_k, cache_sel_v, state_win_k, state_win_v,
           state_pool, page_table, c_prompt, c_sample, w_ada, b_ada, norm_gains, ffn1_wi, ffn1_wo, ffn2_wi,
           ffn2_wo, w_in, w_out, cmp_w, pool_w, pool_scale):
    depth = w_ada.shape[0]
    batch, seq, d = x_prompt.shape
    n_seq, n_tok, _ = x_sample.shape
    n_pages = page_table.shape[1]
    page = cache_cmp_k.shape[2]
    past = n_pages * page
    tm, tq, tk = 512, 128, 512
    assert depth == 1 and seq % tm == 0 and seq // L_SEL == LANES and past // L_SEL == LANES
    assert n_tok <= _TP and past % L_SEL == 0 and past >= max(POOL_WINDOWS) and seq >= WINDOW + tq
    assert state_pool.shape[2] == max(POOL_WINDOWS) - 1 and cache_cmp_k.shape[3:] == (N_KV, HEAD_DIM)
    l = 0
    n_rows_s = n_seq * n_tok
    tps = seq // tm

    mod = _modulation(jnp.concatenate([c_sample, c_prompt], axis=0), w_ada[l], b_ada[l])
    mod_p = mod[:, n_seq:n_seq + batch, None, :]
    mod_s = jnp.tile(mod[:, :n_seq], (1, n_tok, 1))
    gains = norm_gains[l]
    wi1, wo1 = ffn1_wi[l].astype(BF16), ffn1_wo[l].astype(BF16)
    wi2, wo2 = ffn2_wi[l].astype(BF16), ffn2_wo[l].astype(BF16)
    w_all = _prep_w_in(w_in[l])
    w_o = w_out[l].astype(BF16)
    pw = pool_w[l].astype(BF16)
    ps = pool_scale[l].reshape(1, -1)
    cmp_rows_p = jnp.tile(cmp_w[l][:, :, None], (1, tm // L_CMP, KV_W))

    xp = x_prompt.reshape(batch * seq, d)
    xs = jnp.swapaxes(x_sample, 0, 1).reshape(n_rows_s, d)

    xp = _ffn(xp, mod_p, (0, 1, 2), gains[0:2], wi1, wo1, tm=tm, per_row=False, tiles_per_seq=tps)
    xs = _ffn(xs, mod_s, (0, 1, 2), gains[0:2], wi1, wo1, tm=n_rows_s, per_row=True, tiles_per_seq=1)

    (kcr, vcr, ks, vs, kw, vw, kc, vc, qp, gp, ksa, vsa, kwb, vwa, yp_p, u_tail) = _inproj_prompt(
        xp, mod_p, (3, 4), gains[2:3], w_all, cmp_rows_p, pw, ps, batch=batch, seq=seq, tm=tm)
    on_p = _attn_prompt(qp, gp, _even_odd(kc), _even_odd(vc), ksa, vsa, kwb, vwa, tq=tq, tk=tk)

    hist = jnp.swapaxes(state_pool[l], 0, 1)
    kv_s, q_s, g_s, yp_s, u_s = _inproj_sample(
        xs, mod_s, (3, 4), gains[2:3], w_all, hist, pw, ps, n_seq=n_seq, n_tok=n_tok, past_len=past)
    by_seq = lambda a: jnp.swapaxes(a.reshape(n_tok, n_seq, a.shape[-1]), 0, 1)
    new_s = by_seq(kv_s)
    cols = lambda a: jnp.transpose(a, (0, 2, 3, 1)).reshape(a.shape[0], KV_W, a.shape[1])
    on_s = _attn_sample(page_table, by_seq(q_s), by_seq(g_s), new_s, cols(state_win_k[l]), cols(state_win_v[l]),
                        _cmp_placement(cmp_w[l, 0], past // 2), _cmp_placement(cmp_w[l, 1], past // 2),
                        _even_odd_perm(past // L_CMP),
                        cols(cache_cmp_k[l]), cols(cache_cmp_v[l]), cols(cache_sel_k[l]), cols(cache_sel_v[l]))
    on_s = jnp.swapaxes(on_s, 0, 1).reshape(n_rows_s, -1)

    xp = _out_ffn(on_p.reshape(batch * seq, -1), yp_p.reshape(batch * seq, -1), xp,
                  mod_p, (5, 6, 7, 8), gains[3:6], w_o, wi2, wo2, tm=tm, per_row=False, tiles_per_seq=tps)
    xs = _out_ffn(on_s, yp_s, xs, mod_s, (5, 6, 7, 8), gains[3:6], w_o, wi2, wo2,
                  tm=n_rows_s, per_row=True, tiles_per_seq=1)

    heads = lambda a: a.reshape(a.shape[:-1] + (N_KV, HEAD_DIM))[None]
    keep = min(WINDOW, seq)
    n_state = state_pool.shape[2]
    y_prompt = xp.reshape(batch, seq, d)
    y_sample = jnp.swapaxes(xs.reshape(n_tok, n_seq, d), 0, 1)
    p_out = (heads(kcr), heads(vcr), heads(ks), heads(vs), heads(kw[:, seq - keep:]), heads(vw[:, seq - keep:]),
             u_tail[:, _HALO - n_state:][None])
    new4 = [heads(new_s[..., j * KV_W:(j + 1) * KV_W]) for j in range(6)]
    keep_s = min(WINDOW, state_win_k.shape[2] + n_tok)
    win_k = jnp.concatenate([state_win_k, new4[4]], axis=2)
    win_v = jnp.concatenate([state_win_v, new4[5]], axis=2)
    pool_s = jnp.concatenate([state_pool[l], by_seq(u_s)], axis=1)
    s_out = (new4[0], new4[1], new4[2], new4[3], win_k[:, :, win_k.shape[2] - keep_s:],
             win_v[:, :, win_v.shape[2] - keep_s:], pool_s[:, pool_s.shape[1] - n_state:][None])
    return (y_prompt, y_sample) + p_out + s_out
```

```python
import functools

import jax
import jax.numpy as jnp
from jax import lax
from jax.experimental import pallas as pl
from jax.experimental.pallas import tpu as pltpu

HEAD_DIM = 64
N_KV = 2
L_CMP = 32
L_SEL = 64
N_SEL = 16
WINDOW = 512
POOL_WINDOWS = (2, 4, 8, 16)
N_MOD = 9
RMS_EPS = 1e-6
NEG = -1e30

LANES = 128
SUBLANES = 8
VMEM_LIMIT_BYTES = 56 * 1024 * 1024

F32 = jnp.float32
BF16 = jnp.bfloat16

KV_W = N_KV * HEAD_DIM
_NT = (((1,), (1,)), ((), ()))


def _rms(x, g):
    return x * lax.rsqrt(jnp.mean(x * x, axis=-1, keepdims=True) + RMS_EPS) * g


def _silu(a):
    return a * jax.nn.sigmoid(a)


def _params(sem):
    return pltpu.CompilerParams(dimension_semantics=sem, vmem_limit_bytes=VMEM_LIMIT_BYTES)


def _resident(shape):
    return pl.BlockSpec(shape, lambda *_: (0,) * len(shape), pipeline_mode=pl.Buffered(1))


def _mod_body(c_ref, w_ref, b_ref, o_ref):
    a = _silu(c_ref[...]).astype(BF16)
    o_ref[...] = jnp.dot(a, w_ref[...].astype(BF16), preferred_element_type=F32) + b_ref[...]


def _modulation(c, w_ada, b_ada):
    n, d = c.shape
    n_pad = -(-n // SUBLANES) * SUBLANES
    c = jnp.pad(c, ((0, n_pad - n), (0, 0)))
    return pl.pallas_call(
        _mod_body,
        out_shape=jax.ShapeDtypeStruct((N_MOD, n_pad, d), F32),
        grid=(N_MOD,),
        in_specs=[pl.BlockSpec((n_pad, d), lambda j: (0, 0)),
                  pl.BlockSpec((d, d), lambda j: (0, j)),
                  pl.BlockSpec((1, d), lambda j: (0, j))],
        out_specs=pl.BlockSpec((None, n_pad, d), lambda j: (j, 0, 0)),
        compiler_params=_params(("arbitrary",)),
        name="adaln_modulation",
    )(c, w_ada, b_ada.reshape(1, -1))


def _ffn_chunks(d_ff):
    chunks, o = [], 0
    while o < d_ff:
        w = min(512, d_ff - o)
        chunks.append((o, w))
        o += w
    return chunks


def _swiglu(h, wi_ref, wo_ref, act_ref, d_ff):
    for o, w in _ffn_chunks(d_ff):
        a = jnp.dot(h, wi_ref[:, o:o + w], preferred_element_type=F32)
        b = jnp.dot(h, wi_ref[:, d_ff + o:d_ff + o + w], preferred_element_type=F32)
        act_ref[:, o:o + w] = (_silu(a) * b).astype(BF16)
    return jnp.dot(act_ref[...], wo_ref[...], preferred_element_type=F32)


def _ffn_body(x_ref, sh_ref, sc_ref, gt_ref, gains_ref, wi_ref, wo_ref, o_ref, act_ref, *, d_ff):
    x = x_ref[...]
    h = (_rms(x, gains_ref[0:1, :]) * (1 + sc_ref[...]) + sh_ref[...]).astype(BF16)
    y = _swiglu(h, wi_ref, wo_ref, act_ref, d_ff)
    o_ref[...] = x + 0.5 * gt_ref[...] * _rms(y, gains_ref[1:2, :])


def _mod_spec(k, per_row, tm, d, tiles_per_seq):
    if per_row:
        return pl.BlockSpec((None, tm, d), lambda i: (k, i, 0))
    return pl.BlockSpec((None, None, 1, d), lambda i: (k, i // tiles_per_seq, 0, 0))


def _ffn(x, mod, ks, gains, wi, wo, *, tm, per_row, tiles_per_seq):
    r, d = x.shape
    d_ff = wo.shape[0]
    return pl.pallas_call(
        functools.partial(_ffn_body, d_ff=d_ff),
        out_shape=jax.ShapeDtypeStruct((r, d), F32),
        grid=(r // tm,),
        in_specs=[pl.BlockSpec((tm, d), lambda i: (i, 0))]
                 + [_mod_spec(k, per_row, tm, d, tiles_per_seq) for k in ks]
                 + [_resident((2, d)), _resident(wi.shape), _resident(wo.shape)],
        out_specs=pl.BlockSpec((tm, d), lambda i: (i, 0)),
        scratch_shapes=[pltpu.VMEM((tm, d_ff), BF16)],
        compiler_params=_params(("arbitrary",)),
        name="ffn",
    )(x, *([mod] * len(ks)), gains, wi, wo)


_QW = 8 * LANES
_O_KC, _O_VC, _O_KS, _O_VS, _O_KW, _O_VW = (_QW + i * KV_W for i in range(6))
_O_G = _QW + 6 * KV_W
_O_U = _O_G + LANES
_POOL_W = len(POOL_WINDOWS) * LANES
_IN_COLS = _O_U + _POOL_W
_HALO = 16


def _prep_w_in(w_in):
    d = w_in.shape[0]
    n_heads = (_QW // LANES)
    hpg = n_heads // N_KV
    q = w_in[:, :n_heads * HEAD_DIM].reshape(d, n_heads, 1, HEAD_DIM)
    own_group = (jnp.arange(n_heads) // hpg)[:, None] == jnp.arange(N_KV)[None, :]
    slot = jnp.where(own_group[None, :, :, None], q, 0.0)
    o = n_heads * HEAD_DIM
    kv = w_in[:, o:o + 6 * KV_W]
    o += 6 * KV_W
    n_g = n_heads * 3
    g = jnp.pad(w_in[:, o:o + n_g], ((0, 0), (0, LANES - n_g)))
    u = w_in[:, o + n_g:]
    return jnp.concatenate([slot.reshape(d, _QW), kv, g, u], axis=1).astype(BF16)


def _project(x_ref, sh_ref, sc_ref, gain_ref, w_ref):
    h = (_rms(x_ref[...], gain_ref[...]) * (1 + sc_ref[...]) + sh_ref[...]).astype(BF16)
    return jnp.dot(h, w_ref[...], preferred_element_type=F32)


def _pool_out(d, pw_ref, ps_ref):
    parts = [jnp.dot(d[:, g * LANES:(g + 1) * LANES].astype(BF16), pw_ref[g], preferred_element_type=F32)
             for g in range(len(POOL_WINDOWS))]
    return (jnp.concatenate(parts, axis=1) * ps_ref[...]).astype(BF16)


def _inproj_prompt_body(x_ref, sh_ref, sc_ref, gain_ref, w_ref, cw_ref, pw_ref, ps_ref,
                        kcr_ref, vcr_ref, ks_ref, vs_ref, kw_ref, vw_ref, kc_ref, vc_ref,
                        q_ref, g_ref, ksa_ref, vsa_ref, kwb_ref, vwa_ref, yp_ref, ul_ref,
                        ext_ref, *, tm):
    i = pl.program_id(1)
    z = _project(x_ref, sh_ref, sc_ref, gain_ref, w_ref)
    kcr, vcr = z[:, _O_KC:_O_KC + KV_W], z[:, _O_VC:_O_VC + KV_W]
    ks, vs = z[:, _O_KS:_O_KS + KV_W], z[:, _O_VS:_O_VS + KV_W]
    kw, vw = z[:, _O_KW:_O_KW + KV_W], z[:, _O_VW:_O_VW + KV_W]
    kcr_ref[...], vcr_ref[...] = kcr, vcr
    ks_ref[...], vs_ref[...] = ks, vs
    kw_ref[...], vw_ref[...] = kw, vw
    kc_ref[...] = (kcr * cw_ref[0]).reshape(tm // L_CMP, L_CMP, KV_W).sum(axis=1)
    vc_ref[...] = (vcr * cw_ref[1]).reshape(tm // L_CMP, L_CMP, KV_W).sum(axis=1)
    q_ref[...] = (z[:, :_QW] * (HEAD_DIM ** -0.5)).astype(BF16)
    g_ref[...] = jax.nn.sigmoid(z[:, _O_G:_O_G + LANES])
    pos = i * tm + lax.broadcasted_iota(jnp.int32, (tm, LANES), 0)
    blk = lax.broadcasted_iota(jnp.int32, (tm, LANES), 1)
    onehot = jnp.where(pos // L_SEL == blk, 1.0, 0.0).astype(BF16)
    ones = jnp.ones((tm, LANES), BF16)
    ksa_ref[...] = jnp.concatenate([ks.astype(BF16), onehot], axis=1)
    vsa_ref[...] = jnp.concatenate([vs.astype(BF16), ones], axis=1)
    kwb_ref[...] = kw.astype(BF16)
    vwa_ref[...] = jnp.concatenate([vw.astype(BF16), ones], axis=1)
    u = z[:, _O_U:_O_U + _POOL_W]

    @pl.when(i == 0)
    def _():
        ext_ref[0:_HALO, :] = jnp.zeros((_HALO, _POOL_W), F32)

    ext_ref[_HALO:, :] = u
    tpos = i * tm + lax.broadcasted_iota(jnp.int32, (tm, LANES), 0)
    diffs = []
    for g, w in enumerate(POOL_WINDOWS):
        s = ext_ref[:, g * LANES:(g + 1) * LANES]
        k = 1
        while k < w:
            s = s + pltpu.roll(s, k, 0)
            k *= 2
        cnt = jnp.minimum(w, tpos + 1).astype(F32)
        diffs.append(s[_HALO:] / cnt - u[:, g * LANES:(g + 1) * LANES])
    yp_ref[...] = _pool_out(jnp.concatenate(diffs, axis=1), pw_ref, ps_ref)
    tail = u[tm - _HALO:, :]
    ext_ref[0:_HALO, :] = tail
    ul_ref[...] = tail


def _inproj_prompt(x, mod, ks, gain, w_all, cmp_rows, pool_w, pool_scale, *, batch, seq, tm):
    d = x.shape[1]
    tps = seq // tm
    row = lambda c: pl.BlockSpec((None, tm, c), lambda b, i: (b, i, 0))
    mods = [pl.BlockSpec((None, None, 1, d), lambda b, i, k=k: (k, b, 0, 0)) for k in ks]
    f32o = lambda c: jax.ShapeDtypeStruct((batch, seq, c), F32)
    bf16o = lambda c: jax.ShapeDtypeStruct((batch, seq, c), BF16)
    out_shape = ([f32o(KV_W)] * 6
                 + [jax.ShapeDtypeStruct((batch, seq // L_CMP, KV_W), F32)] * 2
                 + [bf16o(_QW), f32o(LANES), bf16o(2 * LANES), bf16o(2 * LANES), bf16o(KV_W), bf16o(2 * LANES),
                    bf16o(_POOL_W), jax.ShapeDtypeStruct((batch, _HALO, _POOL_W), F32)])
    out_specs = ([row(KV_W)] * 6
                 + [pl.BlockSpec((None, tm // L_CMP, KV_W), lambda b, i: (b, i, 0))] * 2
                 + [row(_QW), row(LANES), row(2 * LANES), row(2 * LANES), row(KV_W), row(2 * LANES),
                    row(_POOL_W), pl.BlockSpec((None, _HALO, _POOL_W), lambda b, i: (b, 0, 0))])
    return pl.pallas_call(
        functools.partial(_inproj_prompt_body, tm=tm),
        out_shape=out_shape,
        grid=(batch, tps),
        in_specs=[pl.BlockSpec((tm, d), lambda b, i: (b * tps + i, 0))] + mods
                 + [_resident((1, d)), _resident(w_all.shape), _resident(cmp_rows.shape),
                    _resident(pool_w.shape), _resident(pool_scale.shape)],
        out_specs=out_specs,
        scratch_shapes=[pltpu.VMEM((_HALO + tm, _POOL_W), F32)],
        compiler_params=_params(("arbitrary", "arbitrary")),
        name="inproj_prompt",
    )(x, *([mod] * len(ks)), gain, w_all, cmp_rows, pool_w, pool_scale)


def _inproj_sample_body(x_ref, sh_ref, sc_ref, gain_ref, w_ref, hist_ref, pw_ref, ps_ref,
                        kv_ref, q_ref, g_ref, yp_ref, u_ref, *, n_seq, n_tok, past_len):
    z = _project(x_ref, sh_ref, sc_ref, gain_ref, w_ref)
    kv_ref[...] = z[:, _O_KC:_O_KC + 6 * KV_W]
    q_ref[...] = (z[:, :_QW] * (HEAD_DIM ** -0.5)).astype(BF16)
    g_ref[...] = jax.nn.sigmoid(z[:, _O_G:_O_G + LANES])
    u = z[:, _O_U:_O_U + _POOL_W]
    u_ref[...] = u
    n_hist = hist_ref.shape[0]
    slabs = [hist_ref[j] for j in range(n_hist)] + [u[t * n_seq:(t + 1) * n_seq] for t in range(n_tok)]
    rows = []
    for t in range(n_tok):
        diffs = []
        for g, w in enumerate(POOL_WINDOWS):
            lanes = slice(g * LANES, (g + 1) * LANES)
            s = slabs[n_hist + t][:, lanes]
            for j in range(1, w):
                s = s + slabs[n_hist + t - j][:, lanes]
            diffs.append(s / float(min(w, past_len + t + 1)) - slabs[n_hist + t][:, lanes])
        rows.append(jnp.concatenate(diffs, axis=1))
    yp_ref[...] = _pool_out(jnp.concatenate(rows, axis=0), pw_ref, ps_ref)


def _inproj_sample(x, mod, ks, gain, w_all, hist, pool_w, pool_scale, *, n_seq, n_tok, past_len):
    r, d = x.shape
    full = lambda a: pl.BlockSpec(a.shape, lambda i: (0,) * a.ndim)
    out_shape = [jax.ShapeDtypeStruct((r, 6 * KV_W), F32), jax.ShapeDtypeStruct((r, _QW), BF16),
                 jax.ShapeDtypeStruct((r, LANES), F32), jax.ShapeDtypeStruct((r, _POOL_W), BF16),
                 jax.ShapeDtypeStruct((r, _POOL_W), F32)]
    rest = (gain, w_all, hist, pool_w, pool_scale)
    return pl.pallas_call(
        functools.partial(_inproj_sample_body, n_seq=n_seq, n_tok=n_tok, past_len=past_len),
        out_shape=out_shape,
        grid=(1,),
        in_specs=[full(x)] + [_mod_spec(k, True, r, d, 1) for k in ks] + [full(a) for a in rest],
        out_specs=[pl.BlockSpec(s.shape, lambda i: (0, 0)) for s in out_shape],
        compiler_params=_params(("arbitrary",)),
        name="inproj_sample",
    )(x, *([mod] * len(ks)), *rest)


_N_FORCED = 3


def _select_bias(v, n_pick):
    vt = v.T
    blk = lax.broadcasted_iota(jnp.int32, vt.shape, 0).astype(F32)
    always = vt == jnp.inf
    bias = jnp.where(always, 0.0, NEG)
    vt = jnp.where(always, -jnp.inf, vt)
    for _ in range(n_pick):
        m = jnp.max(vt, axis=0, keepdims=True)
        first = jnp.min(jnp.where(vt == m, blk, float(LANES)), axis=0, keepdims=True)
        hit = blk == first
        bias = jnp.where(hit, 0.0, bias)
        vt = jnp.where(hit, -jnp.inf, vt)
    return bias.T


def _rep(m, n):
    return jnp.concatenate([m] * n, axis=1) if n > 1 else m


def _rowmax(s):
    return jnp.broadcast_to(jnp.max(s, axis=-1, keepdims=True), (s.shape[0], LANES))


def _rowsum(s):
    return jnp.broadcast_to(jnp.sum(s, axis=-1, keepdims=True), (s.shape[0], LANES))


def _mask_heads(s, mask, rows, fill):
    n, c = s.shape[0] // rows, s.shape[1]
    return jnp.where(mask[None], s.reshape(n, rows, c), fill).reshape(n * rows, c)


def _gate_mix(gates, o_cmp, o_sel, o_win, rows):
    n_heads = o_cmp.shape[0] // rows
    hpg = n_heads // N_KV
    parts = []
    for h in range(n_heads):
        g = h // hpg
        lanes = slice(g * HEAD_DIM, (g + 1) * HEAD_DIM)
        rs = slice(h * rows, (h + 1) * rows)
        parts.append(gates[:, 3 * h:3 * h + 1] * o_cmp[rs, lanes]
                     + gates[:, 3 * h + 1:3 * h + 2] * o_sel[rs, lanes]
                     + gates[:, 3 * h + 2:3 * h + 3] * o_win[rs, lanes])
    return jnp.concatenate(parts, axis=1)


def _cmp_attend(qc, kc, vc, valid, n_heads, rows, transposed=False):
    nc = kc.shape[1] if transposed else kc.shape[0]
    if transposed:
        s = jnp.dot(qc, kc, preferred_element_type=F32)
    else:
        s = lax.dot_general(qc, kc, _NT, preferred_element_type=F32)
    if valid is not None:
        s = _mask_heads(s, valid, rows, NEG)
    e = jnp.exp(s - _rep(_rowmax(s), nc // LANES))
    p = e / _rep(_rowsum(e), nc // LANES)
    if valid is not None:
        p = _mask_heads(p, valid, rows, 0.0)
    pb = p.astype(BF16)
    if transposed:
        o = lax.dot_general(pb, vc, _NT, preferred_element_type=F32)
    else:
        o = jnp.dot(pb, vc, preferred_element_type=F32)
    hpg = n_heads // N_KV
    imps = []
    for g in range(N_KV):
        ph = p[g * hpg * rows:(g * hpg + 1) * rows]
        for h in range(1, hpg):
            ph = ph + p[(g * hpg + h) * rows:(g * hpg + h + 1) * rows]
        imps.append(ph[:, :nc // 2] + ph[:, nc // 2:])
    return o, imps


_ROW_SPLIT = 2


def _attn_prompt_body(q_ref, g_ref, kc_ref, vc_ref, ksa_ref, vsa_ref, kwb_ref, vwa_ref, o_ref,
                      qa_ref, m_ref, acc_ref, s_ref, *, tq, tk, n_heads):
    t0 = pl.program_id(1) * tq
    rows = n_heads * tq
    hpg = n_heads // N_KV
    hr = rows // _ROW_SPLIT
    for h in range(n_heads):
        qa_ref[h * tq:(h + 1) * tq, 0:LANES] = q_ref[:, h * LANES:(h + 1) * LANES]
    qc = qa_ref[:, 0:LANES]
    qpos = t0 + lax.broadcasted_iota(jnp.int32, (tq, 1), 0)

    nc = kc_ref.shape[0]
    col = lax.broadcasted_iota(jnp.int32, (tq, nc), 1)
    cblk = jnp.where(col < nc // 2, 2 * col, 2 * (col - nc // 2) + 1)
    valid = (cblk + 1) * L_CMP - 1 <= qpos
    o_cmp, imps = _cmp_attend(qc, kc_ref[...], vc_ref[...], valid, n_heads, tq)

    blk = lax.broadcasted_iota(jnp.int32, (tq, LANES), 1)
    cur = qpos // L_SEL
    forced = (blk == 0) | (blk == cur) | (blk == cur - 1)
    started = blk * L_SEL <= qpos
    v = [jnp.where(forced, jnp.inf, jnp.where(started, imp, NEG)) for imp in imps]
    bias = _select_bias(jnp.concatenate(v, axis=0), N_SEL - _N_FORCED)
    for h in range(n_heads):
        g = h // hpg
        qa_ref[h * tq:(h + 1) * tq, LANES:2 * LANES] = bias[g * tq:(g + 1) * tq].astype(BF16)

    span = WINDOW + tq
    w0 = pl.multiple_of(jnp.maximum(t0 - WINDOW, 0), tq)
    kposw = w0 + lax.broadcasted_iota(jnp.int32, (tq, span), 1)
    band = (kposw <= qpos) & (kposw > qpos - WINDOW)
    o_wins = []
    for r in range(_ROW_SPLIT):
        s = lax.dot_general(qa_ref[r * hr:(r + 1) * hr, 0:LANES], kwb_ref[pl.ds(w0, span), :], _NT,
                            preferred_element_type=F32)
        s = _mask_heads(s, band, tq, NEG)
        p = jnp.exp(s - _rep(_rowmax(s), span // LANES)).astype(BF16)
        accw = jnp.dot(p, vwa_ref[pl.ds(w0, span), :], preferred_element_type=F32)
        o_wins.append(accw[:, 0:LANES] / accw[:, LANES:2 * LANES])
    o_win = jnp.concatenate(o_wins, axis=0)

    m_ref[...] = jnp.full(m_ref.shape, -jnp.inf, F32)
    acc_ref[...] = jnp.zeros(acc_ref.shape, F32)

    def scores(kt, slot):
        ks0 = pl.multiple_of(kt * tk, tk)
        s_ref[slot] = lax.dot_general(qa_ref[...], ksa_ref[pl.ds(ks0, tk), :], _NT, preferred_element_type=F32)

    def accumulate(kt, slot, causal):
        ks0 = pl.multiple_of(kt * tk, tk)
        s = s_ref[slot]
        if causal:
            kpos = ks0 + lax.broadcasted_iota(jnp.int32, (tq, tk), 1)
            s = _mask_heads(s, kpos <= qpos, tq, NEG)
        m_prev = m_ref[...]
        m_new = jnp.maximum(m_prev, jnp.max(s, axis=-1, keepdims=True))
        alpha = jnp.exp(m_prev - m_new)
        p = jnp.exp(s - _rep(m_new, tk // LANES)).astype(BF16)
        pv = jnp.dot(p, vsa_ref[pl.ds(ks0, tk), :], preferred_element_type=F32)
        acc_ref[...] = acc_ref[...] * _rep(alpha, 2) + pv
        m_ref[...] = m_new

    n_full = t0 // tk
    scores(0, 0)

    def tile_pair(jj, carry):
        j = 2 * jj
        scores(j + 1, 1)
        accumulate(j, 0, False)
        scores(j + 2, 0)
        accumulate(j + 1, 1, False)
        return carry

    lax.fori_loop(0, n_full // 2, tile_pair, 0)

    @pl.when(n_full % 2 == 1)
    def _():
        scores(n_full, 1)
        accumulate(n_full - 1, 0, False)
        accumulate(n_full, 1, True)

    @pl.when(n_full % 2 == 0)
    def _():
        accumulate(n_full, 0, True)
    acc = acc_ref[...]
    o_sel = acc[:, 0:LANES] / acc[:, LANES:2 * LANES]

    o_ref[...] = _gate_mix(g_ref[...], o_cmp, o_sel, o_win, tq).astype(BF16)


def _attn_prompt(q, gates, kc, vc, ksa, vsa, kwb, vwa, *, tq, tk):
    batch, seq, qw = q.shape
    n_heads = qw // LANES
    per_b = lambda a: pl.BlockSpec((None,) + a.shape[1:], lambda b, i: (b,) + (0,) * (a.ndim - 1))
    row = lambda c: pl.BlockSpec((None, tq, c), lambda b, i: (b, i, 0))
    return pl.pallas_call(
        functools.partial(_attn_prompt_body, tq=tq, tk=tk, n_heads=n_heads),
        out_shape=jax.ShapeDtypeStruct((batch, seq, n_heads * HEAD_DIM), BF16),
        grid=(batch, seq // tq),
        in_specs=[row(qw), row(LANES), per_b(kc), per_b(vc), per_b(ksa), per_b(vsa), per_b(kwb), per_b(vwa)],
        out_specs=row(n_heads * HEAD_DIM),
        scratch_shapes=[pltpu.VMEM((n_heads * tq, 2 * LANES), BF16),
                        pltpu.VMEM((n_heads * tq, LANES), F32),
                        pltpu.VMEM((n_heads * tq, 2 * LANES), F32),
                        pltpu.VMEM((2, n_heads * tq, tk), F32)],
        compiler_params=_params(("arbitrary", "arbitrary")),
        name="attn_prompt",
    )(q, gates, kc, vc, ksa, vsa, kwb, vwa)


_TP = SUBLANES
_KEY_CHUNKS = 4
_DMA_UNROLL = 4


def _attn_sample_body(pt_ref, q_ref, g_ref, new_ref, swk_ref, swv_ref, wck_ref, wcv_ref, perm_ref,
                      cck_hbm, ccv_hbm, csk_hbm, csv_hbm, o_ref,
                      bufs, sems, cs_ref, ka_ref, va_ref, kn_ref, vn_ref, kwn_ref, vwn_ref, vw_ref, qa_ref, s_ref,
                      *, n_heads, n_tok, n_pages, page):
    i = pl.program_id(0)
    n_seq = pl.num_programs(0)
    caches = (cck_hbm, ccv_hbm, csk_hbm, csv_hbm)
    hpg = n_heads // N_KV
    rows = n_heads * _TP
    past = n_pages * page
    wb = swk_ref.shape[1]

    def page_copy(a, seq, p):
        return pltpu.make_async_copy(caches[a].at[pt_ref[seq, p]], bufs.at[a, p], sems.at[a])

    def start(arrays, seq):
        def body(p, c):
            for a in arrays:
                page_copy(a, seq, p).start(priority=a % 2)
            return c
        lax.fori_loop(0, n_pages, body, 0, unroll=_DMA_UNROLL)

    def wait(a):
        def body(p, c):
            page_copy(a, 0, p).wait()
            return c
        lax.fori_loop(0, n_pages, body, 0, unroll=True)

    def cast_pages(a, dst):
        for p in range(n_pages):
            dst[0:KV_W, p * page:(p + 1) * page] = bufs[a, p].astype(BF16)

    @pl.when(i == 0)
    def _():
        start((0, 1), 0)
        start((2, 3), 0)
        blk = lax.broadcasted_iota(jnp.int32, (LANES, past), 0)
        kpos = lax.broadcasted_iota(jnp.int32, (LANES, past), 1)
        ka_ref[KV_W:, :] = jnp.where(kpos // L_SEL == blk, 1.0, 0.0).astype(BF16)
        vw_ref[KV_W:, :] = jnp.ones((LANES, wb), BF16)
        kn_ref[...] = jnp.zeros(kn_ref.shape, BF16)
        vn_ref[...] = jnp.zeros(vn_ref.shape, BF16)
        kwn_ref[...] = jnp.zeros(kwn_ref.shape, BF16)
        vwn_ref[...] = jnp.zeros(vwn_ref.shape, BF16)
        qa_ref[...] = jnp.zeros(qa_ref.shape, BF16)

    for h in range(n_heads):
        qa_ref[h * _TP:h * _TP + n_tok, 0:LANES] = q_ref[:, h * LANES:(h + 1) * LANES]
    qc = qa_ref[:, 0:LANES]
    tok = lax.broadcasted_iota(jnp.int32, (_TP, 1), 0)

    wait(0)
    wait(1)
    cast_pages(0, cs_ref.at[0])
    cast_pages(1, cs_ref.at[1])

    @pl.when(i + 1 < n_seq)
    def _():
        start((0, 1), i + 1)

    def summaries(a, w_ref):
        half = past // 2
        nat = jnp.concatenate(
            [jnp.dot(cs_ref[a, :, j * half:(j + 1) * half], w_ref[...], preferred_element_type=F32)
             for j in range(2)], axis=1).astype(BF16)
        return jnp.dot(nat, perm_ref[...], preferred_element_type=F32).astype(BF16)

    kct = summaries(0, wck_ref)
    vct = summaries(1, wcv_ref)
    o_cmp, imps = _cmp_attend(qc, kct, vct, None, n_heads, _TP, transposed=True)

    n_blk = past // L_SEL
    blk = lax.broadcasted_iota(jnp.int32, (_TP, LANES), 1)
    forced = (blk == 0) | (blk == n_blk - 1)
    v = jnp.concatenate([jnp.where(forced, jnp.inf, imp) for imp in imps]
                        + [jnp.full((LANES - N_KV * _TP, LANES), NEG, F32)], axis=0)
    bias = _select_bias(v, N_SEL - _N_FORCED)
    for h in range(n_heads):
        g = h // hpg
        qa_ref[h * _TP:(h + 1) * _TP, LANES:] = bias[g * _TP:(g + 1) * _TP].astype(BF16)

    wait(2)
    wait(3)
    cast_pages(2, ka_ref)
    cast_pages(3, va_ref)

    @pl.when(i + 1 < n_seq)
    def _():
        start((2, 3), i + 1)

    new = new_ref[...]
    kn_ref[0:n_tok, :] = new[:, 2 * KV_W:3 * KV_W].astype(BF16)
    vn_ref[0:n_tok, 0:LANES] = new[:, 3 * KV_W:4 * KV_W].astype(BF16)
    vn_ref[0:n_tok, LANES:] = jnp.ones((n_tok, LANES), BF16)
    jn = lax.broadcasted_iota(jnp.int32, (_TP, LANES), 1)
    new_ok = (jn <= tok) & (jn < n_tok)
    chunk = past // _KEY_CHUNKS
    for c in range(_KEY_CHUNKS):
        ks = slice(c * chunk, (c + 1) * chunk)
        s_ref[:, ks] = jnp.dot(qa_ref[...], ka_ref[:, ks], preferred_element_type=F32)
    s_new = lax.dot_general(qc, kn_ref[...], _NT, preferred_element_type=F32)
    s_new = _mask_heads(s_new, new_ok, _TP, NEG)
    m = _rowmax(s_new)
    for c in range(_KEY_CHUNKS):
        m = jnp.maximum(m, _rowmax(s_ref[:, c * chunk:(c + 1) * chunk]))
    acc = jnp.dot(jnp.exp(s_new - m).astype(BF16), vn_ref[...], preferred_element_type=F32)
    num, den = acc[:, 0:LANES], acc[:, LANES:]
    for c in range(_KEY_CHUNKS):
        ks = slice(c * chunk, (c + 1) * chunk)
        e = jnp.exp(s_ref[:, ks] - _rep(m, chunk // LANES))
        den = den + _rowsum(e)
        num = num + lax.dot_general(e.astype(BF16), va_ref[:, ks], _NT, preferred_element_type=F32)
    o_sel = num / den

    kwn_ref[0:n_tok, :] = new[:, 4 * KV_W:5 * KV_W].astype(BF16)
    vwn_ref[0:n_tok, 0:LANES] = new[:, 5 * KV_W:6 * KV_W].astype(BF16)
    vwn_ref[0:n_tok, LANES:] = jnp.ones((n_tok, LANES), BF16)
    vw_ref[0:KV_W, :] = swv_ref[...].astype(BF16)
    s_wp = jnp.dot(qc, swk_ref[...].astype(BF16), preferred_element_type=F32)
    s_wn = lax.dot_general(qc, kwn_ref[...], _NT, preferred_element_type=F32)
    c = lax.broadcasted_iota(jnp.int32, (_TP, wb), 1)
    s_wp = _mask_heads(s_wp, c - wb > tok - WINDOW, _TP, NEG)
    s_wn = _mask_heads(s_wn, new_ok, _TP, NEG)
    m = jnp.maximum(_rowmax(s_wp), _rowmax(s_wn))
    accw = (lax.dot_general(jnp.exp(s_wp - _rep(m, wb // LANES)).astype(BF16), vw_ref[...], _NT,
                            preferred_element_type=F32)
            + jnp.dot(jnp.exp(s_wn - m).astype(BF16), vwn_ref[...], preferred_element_type=F32))
    o_win = accw[:, 0:LANES] / accw[:, LANES:]

    gates = jnp.concatenate([g_ref[...], jnp.zeros((_TP - n_tok, LANES), F32)], axis=0)
    o_ref[...] = _gate_mix(gates, o_cmp, o_sel, o_win, _TP)[0:n_tok].astype(BF16)


def _attn_sample(page_table, q, gates, new, swk, swv, wck, wcv, perm, cck, ccv, csk, csv):
    n_seq, n_tok, qw = q.shape
    n_heads = qw // LANES
    n_pages = page_table.shape[1]
    page = cck.shape[2]
    past = n_pages * page
    wb = swk.shape[2]
    per_seq = lambda a: pl.BlockSpec((None,) + a.shape[1:], lambda i, pt: (i,) + (0,) * (a.ndim - 1))
    hbm = pl.BlockSpec(memory_space=pl.ANY)
    grid_spec = pltpu.PrefetchScalarGridSpec(
        num_scalar_prefetch=1,
        grid=(n_seq,),
        in_specs=[per_seq(q), per_seq(gates), per_seq(new), per_seq(swk), per_seq(swv),
                  _resident(wck.shape), _resident(wcv.shape), _resident(perm.shape), hbm, hbm, hbm, hbm],
        out_specs=pl.BlockSpec((None, n_tok, n_heads * HEAD_DIM), lambda i, pt: (i, 0, 0)),
        scratch_shapes=[pltpu.VMEM((4, n_pages, KV_W, page), F32),
                        pltpu.SemaphoreType.DMA((4,)),
                        pltpu.VMEM((2, KV_W, past), BF16),
                        pltpu.VMEM((2 * LANES, past), BF16),
                        pltpu.VMEM((KV_W, past), BF16),
                        pltpu.VMEM((LANES, KV_W), BF16),
                        pltpu.VMEM((LANES, 2 * LANES), BF16),
                        pltpu.VMEM((LANES, KV_W), BF16),
                        pltpu.VMEM((LANES, 2 * LANES), BF16),
                        pltpu.VMEM((2 * LANES, wb), BF16),
                        pltpu.VMEM((n_heads * _TP, 2 * LANES), BF16),
                        pltpu.VMEM((n_heads * _TP, past), F32)])
    return pl.pallas_call(
        functools.partial(_attn_sample_body, n_heads=n_heads, n_tok=n_tok, n_pages=n_pages, page=page),
        out_shape=jax.ShapeDtypeStruct((n_seq, n_tok, n_heads * HEAD_DIM), BF16),
        grid_spec=grid_spec,
        compiler_params=_params(("arbitrary",)),
        name="attn_sample",
    )(page_table, q, gates, new, swk, swv, wck, wcv, perm, cck, ccv, csk, csv)


def _cmp_placement(w, n_keys):
    n_blk = n_keys // L_CMP
    hit = (jnp.arange(n_keys) // L_CMP)[:, None] == jnp.arange(n_blk)[None, :]
    return jnp.where(hit, jnp.tile(w, n_blk)[:, None], 0.0).astype(BF16)


def _even_odd_perm(nc):
    src = jnp.arange(nc)
    dst = src // 2 + (src % 2) * (nc // 2)
    return (dst[:, None] == jnp.arange(nc)[None, :]).astype(BF16)


def _out_ffn_body(on_ref, yp_ref, x_ref, g2_ref, sh_ref, sc_ref, gt_ref, gains_ref, wout_ref, wi_ref, wo_ref,
                  o_ref, act_ref, *, d_ff):
    half = on_ref.shape[1]
    y = (jnp.dot(on_ref[...], wout_ref[0:half, :], preferred_element_type=F32)
         + jnp.dot(yp_ref[...], wout_ref[half:, :], preferred_element_type=F32))
    x = x_ref[...] + g2_ref[...] * _rms(y, gains_ref[0:1, :])
    h = (_rms(x, gains_ref[1:2, :]) * (1 + sc_ref[...]) + sh_ref[...]).astype(BF16)
    y = _swiglu(h, wi_ref, wo_ref, act_ref, d_ff)
    o_ref[...] = x + 0.5 * gt_ref[...] * _rms(y, gains_ref[2:3, :])


def _out_ffn(o_nsa, y_pool, x, mod, ks, gains, w_out, wi, wo, *, tm, per_row, tiles_per_seq):
    r, d = x.shape
    d_ff = wo.shape[0]
    row = lambda c: pl.BlockSpec((tm, c), lambda i: (i, 0))
    return pl.pallas_call(
        functools.partial(_out_ffn_body, d_ff=d_ff),
        out_shape=jax.ShapeDtypeStruct((r, d), F32),
        grid=(r // tm,),
        in_specs=[row(o_nsa.shape[1]), row(y_pool.shape[1]), row(d)]
                 + [_mod_spec(k, per_row, tm, d, tiles_per_seq) for k in ks]
                 + [_resident((3, d)), _resident(w_out.shape), _resident(wi.shape), _resident(wo.shape)],
        out_specs=row(d),
        scratch_shapes=[pltpu.VMEM((tm, d_ff), BF16)],
        compiler_params=_params(("arbitrary",)),
        name="out_ffn",
    )(o_nsa, y_pool, x, *([mod] * len(ks)), gains, w_out, wi, wo)


def _even_odd(a):
    return jnp.concatenate([a[:, 0::2], a[:, 1::2]], axis=1).astype(BF16)


def kernel(x_prompt, x_sample, cache_cmp_k, cache_cmp_v, cache_sel_k, cache_sel_v, state_win_k, state_win_v,
           state_pool, page_table, c_prompt, c_sample, w_ada, b_ada, norm_gains, ffn1_wi, ffn1_wo, ffn2_wi,
           ffn2_wo, w_in, w_out, cmp_w, pool_w, pool_scale):
    depth = w_ada.shape[0]
    batch, seq, d = x_prompt.shape
    n_seq, n_tok, _ = x_sample.shape
    n_pages = page_table.shape[1]
    page = cache_cmp_k.shape[2]
    past = n_pages * page
    tm, tq, tk = 512, 128, 512
    tm_ffn = 1024
    assert depth == 1 and seq % tm_ffn == 0 and seq // L_SEL == LANES and past // L_SEL == LANES
    assert n_tok <= _TP and past % L_SEL == 0 and past >= max(POOL_WINDOWS) and seq >= WINDOW + tq
    assert state_pool.shape[2] == max(POOL_WINDOWS) - 1 and cache_cmp_k.shape[3:] == (N_KV, HEAD_DIM)
    l = 0
    n_rows_s = n_seq * n_tok

    mod = _modulation(jnp.concatenate([c_sample, c_prompt], axis=0), w_ada[l], b_ada[l])
    mod_p = mod[:, n_seq:n_seq + batch, None, :]
    mod_s = jnp.tile(mod[:, :n_seq], (1, n_tok, 1))
    gains = norm_gains[l]
    wi1, wo1 = ffn1_wi[l].astype(BF16), ffn1_wo[l].astype(BF16)
    wi2, wo2 = ffn2_wi[l].astype(BF16), ffn2_wo[l].astype(BF16)
    w_all = _prep_w_in(w_in[l])
    w_o = w_out[l].astype(BF16)
    pw = pool_w[l].astype(BF16)
    ps = pool_scale[l].reshape(1, -1)
    cmp_rows_p = jnp.tile(cmp_w[l][:, :, None], (1, tm // L_CMP, KV_W))

    xp = x_prompt.reshape(batch * seq, d)
    xs = jnp.swapaxes(x_sample, 0, 1).reshape(n_rows_s, d)

    xp = _ffn(xp, mod_p, (0, 1, 2), gains[0:2], wi1, wo1, tm=tm_ffn, per_row=False,
              tiles_per_seq=seq // tm_ffn)
    xs = _ffn(xs, mod_s, (0, 1, 2), gains[0:2], wi1, wo1, tm=n_rows_s, per_row=True, tiles_per_seq=1)

    (kcr, vcr, ks, vs, kw, vw, kc, vc, qp, gp, ksa, vsa, kwb, vwa, yp_p, u_tail) = _inproj_prompt(
        xp, mod_p, (3, 4), gains[2:3], w_all, cmp_rows_p, pw, ps, batch=batch, seq=seq, tm=tm)
    on_p = _attn_prompt(qp, gp, _even_odd(kc), _even_odd(vc), ksa, vsa, kwb, vwa, tq=tq, tk=tk)

    hist = jnp.swapaxes(state_pool[l], 0, 1)
    kv_s, q_s, g_s, yp_s, u_s = _inproj_sample(
        xs, mod_s, (3, 4), gains[2:3], w_all, hist, pw, ps, n_seq=n_seq, n_tok=n_tok, past_len=past)
    by_seq = lambda a: jnp.swapaxes(a.reshape(n_tok, n_seq, a.shape[-1]), 0, 1)
    new_s = by_seq(kv_s)
    cols = lambda a: jnp.transpose(a, (0, 2, 3, 1)).reshape(a.shape[0], KV_W, a.shape[1])
    on_s = _attn_sample(page_table, by_seq(q_s), by_seq(g_s), new_s, cols(state_win_k[l]), cols(state_win_v[l]),
                        _cmp_placement(cmp_w[l, 0], past // 2), _cmp_placement(cmp_w[l, 1], past // 2),
                        _even_odd_perm(past // L_CMP),
                        cols(cache_cmp_k[l]), cols(cache_cmp_v[l]), cols(cache_sel_k[l]), cols(cache_sel_v[l]))
    on_s = jnp.swapaxes(on_s, 0, 1).reshape(n_rows_s, -1)

    xp = _out_ffn(on_p.reshape(batch * seq, -1), yp_p.reshape(batch * seq, -1), xp,
                  mod_p, (5, 6, 7, 8), gains[3:6], w_o, wi2, wo2, tm=tm_ffn, per_row=False,
                  tiles_per_seq=seq // tm_ffn)
    xs = _out_ffn(on_s, yp_s, xs, mod_s, (5, 6, 7, 8), gains[3:6], w_o, wi2, wo2,
                  tm=n_rows_s, per_row=True, tiles_per_seq=1)

    heads = lambda a: a.reshape(a.shape[:-1] + (N_KV, HEAD_DIM))[None]
    keep = min(WINDOW, seq)
    n_state = state_pool.shape[2]
    y_prompt = xp.reshape(batch, seq, d)
    y_sample = jnp.swapaxes(xs.reshape(n_tok, n_seq, d), 0, 1)
    p_out = (heads(kcr), heads(vcr), heads(ks), heads(vs), heads(kw[:, seq - keep:]), heads(vw[:, seq - keep:]),
             u_tail[:, _HALO - n_state:][None])
    new4 = [heads(new_s[..., j * KV_W:(j + 1) * KV_W]) for j in range(6)]
    keep_s = min(WINDOW, state_win_k.shape[2] + n_tok)
    win_k = jnp.concatenate([state_win_k, new4[4]], axis=2)
    win_v = jnp.concatenate([state_win_v, new4[5]], axis=2)
    pool_s = jnp.concatenate([state_pool[l], by_seq(u_s)], axis=1)
    s_out = (new4[0], new4[1], new4[2], new4[3], win_k[:, :, win_k.shape[2] - keep_s:],
             win_v[:, :, win_v.shape[2] - keep_s:], pool_s[:, pool_s.shape[1] - n_state:][None])
    return (y_prompt, y_sample) + p_out + s_out
```

```python
import functools

import jax
import jax.numpy as jnp
from jax import lax
from jax.experimental import pallas as pl
from jax.experimental.pallas import tpu as pltpu

HEAD_DIM = 64
N_KV = 2
L_CMP = 32
L_SEL = 64
N_SEL = 16
WINDOW = 512
POOL_WINDOWS = (2, 4, 8, 16)
N_MOD = 9
RMS_EPS = 1e-6
NEG = -1e30

LANES = 128
SUBLANES = 8
VMEM_LIMIT_BYTES = 56 * 1024 * 1024

F32 = jnp.float32
BF16 = jnp.bfloat16

KV_W = N_KV * HEAD_DIM
_NT = (((1,), (1,)), ((), ()))


def _rms(x, g):
    return x * lax.rsqrt(jnp.mean(x * x, axis=-1, keepdims=True) + RMS_EPS) * g


def _silu(a):
    return a * jax.nn.sigmoid(a)


def _params(sem):
    return pltpu.CompilerParams(dimension_semantics=sem, vmem_limit_bytes=VMEM_LIMIT_BYTES)


def _resident(shape):
    return pl.BlockSpec(shape, lambda *_: (0,) * len(shape), pipeline_mode=pl.Buffered(1))


def _mod_body(c_ref, w_ref, b_ref, o_ref):
    a = _silu(c_ref[...]).astype(BF16)
    o_ref[...] = jnp.dot(a, w_ref[...].astype(BF16), preferred_element_type=F32) + b_ref[...]


def _modulation(c, w_ada, b_ada):
    n, d = c.shape
    n_pad = -(-n // SUBLANES) * SUBLANES
    c = jnp.pad(c, ((0, n_pad - n), (0, 0)))
    return pl.pallas_call(
        _mod_body,
        out_shape=jax.ShapeDtypeStruct((N_MOD, n_pad, d), F32),
        grid=(N_MOD,),
        in_specs=[pl.BlockSpec((n_pad, d), lambda j: (0, 0)),
                  pl.BlockSpec((d, d), lambda j: (0, j)),
                  pl.BlockSpec((1, d), lambda j: (0, j))],
        out_specs=pl.BlockSpec((None, n_pad, d), lambda j: (j, 0, 0)),
        compiler_params=_params(("arbitrary",)),
        name="adaln_modulation",
    )(c, w_ada, b_ada.reshape(1, -1))


def _ffn_chunks(d_ff):
    chunks, o = [], 0
    while o < d_ff:
        w = min(512, d_ff - o)
        chunks.append((o, w))
        o += w
    return chunks


def _swiglu(h, wi_ref, wo_ref, act_ref, d_ff):
    for o, w in _ffn_chunks(d_ff):
        a = jnp.dot(h, wi_ref[:, o:o + w], preferred_element_type=F32)
        b = jnp.dot(h, wi_ref[:, d_ff + o:d_ff + o + w], preferred_element_type=F32)
        act_ref[:, o:o + w] = (_silu(a) * b).astype(BF16)
    return jnp.dot(act_ref[...], wo_ref[...], preferred_element_type=F32)


def _ffn_body(x_ref, sh_ref, sc_ref, gt_ref, gains_ref, wi_ref, wo_ref, o_ref, act_ref, *, d_ff):
    x = x_ref[...]
    h = (_rms(x, gains_ref[0:1, :]) * (1 + sc_ref[...]) + sh_ref[...]).astype(BF16)
    y = _swiglu(h, wi_ref, wo_ref, act_ref, d_ff)
    o_ref[...] = x + 0.5 * gt_ref[...] * _rms(y, gains_ref[1:2, :])


def _mod_spec(k, per_row, tm, d, tiles_per_seq):
    if per_row:
        return pl.BlockSpec((None, tm, d), lambda i: (k, i, 0))
    return pl.BlockSpec((None, None, 1, d), lambda i: (k, i // tiles_per_seq, 0, 0))


def _ffn(x, mod, ks, gains, wi, wo, *, tm, per_row, tiles_per_seq):
    r, d = x.shape
    d_ff = wo.shape[0]
    return pl.pallas_call(
        functools.partial(_ffn_body, d_ff=d_ff),
        out_shape=jax.ShapeDtypeStruct((r, d), F32),
        grid=(r // tm,),
        in_specs=[pl.BlockSpec((tm, d), lambda i: (i, 0))]
                 + [_mod_spec(k, per_row, tm, d, tiles_per_seq) for k in ks]
                 + [_resident((2, d)), _resident(wi.shape), _resident(wo.shape)],
        out_specs=pl.BlockSpec((tm, d), lambda i: (i, 0)),
        scratch_shapes=[pltpu.VMEM((tm, d_ff), BF16)],
        compiler_params=_params(("arbitrary",)),
        name="ffn",
    )(x, *([mod] * len(ks)), gains, wi, wo)


_QW = 8 * LANES
_O_KC, _O_VC, _O_KS, _O_VS, _O_KW, _O_VW = (_QW + i * KV_W for i in range(6))
_O_G = _QW + 6 * KV_W
_O_U = _O_G + LANES
_POOL_W = len(POOL_WINDOWS) * LANES
_IN_COLS = _O_U + _POOL_W
_HALO = 16


def _prep_w_in(w_in):
    d = w_in.shape[0]
    n_heads = (_QW // LANES)
    hpg = n_heads // N_KV
    q = w_in[:, :n_heads * HEAD_DIM].reshape(d, n_heads, 1, HEAD_DIM)
    own_group = (jnp.arange(n_heads) // hpg)[:, None] == jnp.arange(N_KV)[None, :]
    slot = jnp.where(own_group[None, :, :, None], q, 0.0)
    o = n_heads * HEAD_DIM
    kv = w_in[:, o:o + 6 * KV_W]
    o += 6 * KV_W
    n_g = n_heads * 3
    g = jnp.pad(w_in[:, o:o + n_g], ((0, 0), (0, LANES - n_g)))
    u = w_in[:, o + n_g:]
    return jnp.concatenate([slot.reshape(d, _QW), kv, g, u], axis=1).astype(BF16)


def _project(x_ref, sh_ref, sc_ref, gain_ref, w_ref):
    h = (_rms(x_ref[...], gain_ref[...]) * (1 + sc_ref[...]) + sh_ref[...]).astype(BF16)
    return jnp.dot(h, w_ref[...], preferred_element_type=F32)


def _pool_out(d, pw_ref, ps_ref):
    parts = [jnp.dot(d[:, g * LANES:(g + 1) * LANES].astype(BF16), pw_ref[g], preferred_element_type=F32)
             for g in range(len(POOL_WINDOWS))]
    return (jnp.concatenate(parts, axis=1) * ps_ref[...]).astype(BF16)


def _inproj_prompt_body(x_ref, sh_ref, sc_ref, gain_ref, w_ref, cw_ref, pw_ref, ps_ref,
                        kcr_ref, vcr_ref, ks_ref, vs_ref, kw_ref, vw_ref, kc_ref, vc_ref,
                        q_ref, g_ref, ksa_ref, vsa_ref, kwb_ref, vwa_ref, yp_ref, ul_ref,
                        ext_ref, *, tm):
    i = pl.program_id(1)
    z = _project(x_ref, sh_ref, sc_ref, gain_ref, w_ref)
    kcr, vcr = z[:, _O_KC:_O_KC + KV_W], z[:, _O_VC:_O_VC + KV_W]
    ks, vs = z[:, _O_KS:_O_KS + KV_W], z[:, _O_VS:_O_VS + KV_W]
    kw, vw = z[:, _O_KW:_O_KW + KV_W], z[:, _O_VW:_O_VW + KV_W]
    kcr_ref[...], vcr_ref[...] = kcr, vcr
    ks_ref[...], vs_ref[...] = ks, vs
    kw_ref[...], vw_ref[...] = kw, vw
    kc_ref[...] = (kcr * cw_ref[0]).reshape(tm // L_CMP, L_CMP, KV_W).sum(axis=1)
    vc_ref[...] = (vcr * cw_ref[1]).reshape(tm // L_CMP, L_CMP, KV_W).sum(axis=1)
    q_ref[...] = (z[:, :_QW] * (HEAD_DIM ** -0.5)).astype(BF16)
    g_ref[...] = jax.nn.sigmoid(z[:, _O_G:_O_G + LANES])
    pos = i * tm + lax.broadcasted_iota(jnp.int32, (tm, LANES), 0)
    blk = lax.broadcasted_iota(jnp.int32, (tm, LANES), 1)
    onehot = jnp.where(pos // L_SEL == blk, 1.0, 0.0).astype(BF16)
    ones = jnp.ones((tm, LANES), BF16)
    ksa_ref[...] = jnp.concatenate([ks.astype(BF16), onehot], axis=1)
    vsa_ref[...] = jnp.concatenate([vs.astype(BF16), ones], axis=1)
    kwb_ref[...] = kw.astype(BF16)
    vwa_ref[...] = jnp.concatenate([vw.astype(BF16), ones], axis=1)
    u = z[:, _O_U:_O_U + _POOL_W]

    @pl.when(i == 0)
    def _():
        ext_ref[0:_HALO, :] = jnp.zeros((_HALO, _POOL_W), F32)

    ext_ref[_HALO:, :] = u
    tpos = i * tm + lax.broadcasted_iota(jnp.int32, (tm, LANES), 0)
    diffs = []
    for g, w in enumerate(POOL_WINDOWS):
        s = ext_ref[:, g * LANES:(g + 1) * LANES]
        k = 1
        while k < w:
            s = s + pltpu.roll(s, k, 0)
            k *= 2
        cnt = jnp.minimum(w, tpos + 1).astype(F32)
        diffs.append(s[_HALO:] / cnt - u[:, g * LANES:(g + 1) * LANES])
    yp_ref[...] = _pool_out(jnp.concatenate(diffs, axis=1), pw_ref, ps_ref)
    tail = u[tm - _HALO:, :]
    ext_ref[0:_HALO, :] = tail
    ul_ref[...] = tail


def _inproj_prompt(x, mod, ks, gain, w_all, cmp_rows, pool_w, pool_scale, *, batch, seq, tm):
    d = x.shape[1]
    tps = seq // tm
    row = lambda c: pl.BlockSpec((None, tm, c), lambda b, i: (b, i, 0))
    mods = [pl.BlockSpec((None, None, 1, d), lambda b, i, k=k: (k, b, 0, 0)) for k in ks]
    f32o = lambda c: jax.ShapeDtypeStruct((batch, seq, c), F32)
    bf16o = lambda c: jax.ShapeDtypeStruct((batch, seq, c), BF16)
    out_shape = ([f32o(KV_W)] * 6
                 + [jax.ShapeDtypeStruct((batch, seq // L_CMP, KV_W), F32)] * 2
                 + [bf16o(_QW), f32o(LANES), bf16o(2 * LANES), bf16o(2 * LANES), bf16o(KV_W), bf16o(2 * LANES),
                    bf16o(_POOL_W), jax.ShapeDtypeStruct((batch, _HALO, _POOL_W), F32)])
    out_specs = ([row(KV_W)] * 6
                 + [pl.BlockSpec((None, tm // L_CMP, KV_W), lambda b, i: (b, i, 0))] * 2
                 + [row(_QW), row(LANES), row(2 * LANES), row(2 * LANES), row(KV_W), row(2 * LANES),
                    row(_POOL_W), pl.BlockSpec((None, _HALO, _POOL_W), lambda b, i: (b, 0, 0))])
    return pl.pallas_call(
        functools.partial(_inproj_prompt_body, tm=tm),
        out_shape=out_shape,
        grid=(batch, tps),
        in_specs=[pl.BlockSpec((tm, d), lambda b, i: (b * tps + i, 0))] + mods
                 + [_resident((1, d)), _resident(w_all.shape), _resident(cmp_rows.shape),
                    _resident(pool_w.shape), _resident(pool_scale.shape)],
        out_specs=out_specs,
        scratch_shapes=[pltpu.VMEM((_HALO + tm, _POOL_W), F32)],
        compiler_params=_params(("arbitrary", "arbitrary")),
        name="inproj_prompt",
    )(x, *([mod] * len(ks)), gain, w_all, cmp_rows, pool_w, pool_scale)


def _inproj_sample_body(x_ref, sh_ref, sc_ref, gain_ref, w_ref, hist_ref, pw_ref, ps_ref,
                        kv_ref, q_ref, g_ref, yp_ref, u_ref, *, n_seq, n_tok, past_len):
    z = _project(x_ref, sh_ref, sc_ref, gain_ref, w_ref)
    kv_ref[...] = z[:, _O_KC:_O_KC + 6 * KV_W]
    q_ref[...] = (z[:, :_QW] * (HEAD_DIM ** -0.5)).astype(BF16)
    g_ref[...] = jax.nn.sigmoid(z[:, _O_G:_O_G + LANES])
    u = z[:, _O_U:_O_U + _POOL_W]
    u_ref[...] = u
    n_hist = hist_ref.shape[0]
    slabs = [hist_ref[j] for j in range(n_hist)] + [u[t * n_seq:(t + 1) * n_seq] for t in range(n_tok)]
    rows = []
    for t in range(n_tok):
        diffs = []
        for g, w in enumerate(POOL_WINDOWS):
            lanes = slice(g * LANES, (g + 1) * LANES)
            s = slabs[n_hist + t][:, lanes]
            for j in range(1, w):
                s = s + slabs[n_hist + t - j][:, lanes]
            diffs.append(s / float(min(w, past_len + t + 1)) - slabs[n_hist + t][:, lanes])
        rows.append(jnp.concatenate(diffs, axis=1))
    yp_ref[...] = _pool_out(jnp.concatenate(rows, axis=0), pw_ref, ps_ref)


def _inproj_sample(x, mod, ks, gain, w_all, hist, pool_w, pool_scale, *, n_seq, n_tok, past_len):
    r, d = x.shape
    full = lambda a: pl.BlockSpec(a.shape, lambda i: (0,) * a.ndim)
    out_shape = [jax.ShapeDtypeStruct((r, 6 * KV_W), F32), jax.ShapeDtypeStruct((r, _QW), BF16),
                 jax.ShapeDtypeStruct((r, LANES), F32), jax.ShapeDtypeStruct((r, _POOL_W), BF16),
                 jax.ShapeDtypeStruct((r, _POOL_W), F32)]
    rest = (gain, w_all, hist, pool_w, pool_scale)
    return pl.pallas_call(
        functools.partial(_inproj_sample_body, n_seq=n_seq, n_tok=n_tok, past_len=past_len),
        out_shape=out_shape,
        grid=(1,),
        in_specs=[full(x)] + [_mod_spec(k, True, r, d, 1) for k in ks] + [full(a) for a in rest],
        out_specs=[pl.BlockSpec(s.shape, lambda i: (0, 0)) for s in out_shape],
        compiler_params=_params(("arbitrary",)),
        name="inproj_sample",
    )(x, *([mod] * len(ks)), *rest)


_N_FORCED = 3


def _select_bias(v, n_pick):
    vt = v.T
    blk = lax.broadcasted_iota(jnp.int32, vt.shape, 0).astype(F32)
    always = vt == jnp.inf
    bias = jnp.where(always, 0.0, NEG)
    vt = jnp.where(always, -jnp.inf, vt)
    for _ in range(n_pick):
        m = jnp.max(vt, axis=0, keepdims=True)
        first = jnp.min(jnp.where(vt == m, blk, float(LANES)), axis=0, keepdims=True)
        hit = blk == first
        bias = jnp.where(hit, 0.0, bias)
        vt = jnp.where(hit, -jnp.inf, vt)
    return bias.T


def _rep(m, n):
    return jnp.concatenate([m] * n, axis=1) if n > 1 else m


def _rowmax(s):
    return jnp.broadcast_to(jnp.max(s, axis=-1, keepdims=True), (s.shape[0], LANES))


def _rowsum(s):
    return jnp.broadcast_to(jnp.sum(s, axis=-1, keepdims=True), (s.shape[0], LANES))


def _mask_heads(s, mask, rows, fill):
    n, c = s.shape[0] // rows, s.shape[1]
    return jnp.where(mask[None], s.reshape(n, rows, c), fill).reshape(n * rows, c)


def _gate_mix(gates, o_cmp, o_sel, o_win, rows):
    n_heads = o_cmp.shape[0] // rows
    hpg = n_heads // N_KV
    parts = []
    for h in range(n_heads):
        g = h // hpg
        lanes = slice(g * HEAD_DIM, (g + 1) * HEAD_DIM)
        rs = slice(h * rows, (h + 1) * rows)
        parts.append(gates[:, 3 * h:3 * h + 1] * o_cmp[rs, lanes]
                     + gates[:, 3 * h + 1:3 * h + 2] * o_sel[rs, lanes]
                     + gates[:, 3 * h + 2:3 * h + 3] * o_win[rs, lanes])
    return jnp.concatenate(parts, axis=1)


def _cmp_attend(qc, kc, vc, valid, n_heads, rows, transposed=False):
    nc = kc.shape[1] if transposed else kc.shape[0]
    if transposed:
        s = jnp.dot(qc, kc, preferred_element_type=F32)
    else:
        s = lax.dot_general(qc, kc, _NT, preferred_element_type=F32)
    if valid is not None:
        s = _mask_heads(s, valid, rows, NEG)
    e = jnp.exp(s - _rep(_rowmax(s), nc // LANES))
    p = e / _rep(_rowsum(e), nc // LANES)
    if valid is not None:
        p = _mask_heads(p, valid, rows, 0.0)
    pb = p.astype(BF16)
    if transposed:
        o = lax.dot_general(pb, vc, _NT, preferred_element_type=F32)
    else:
        o = jnp.dot(pb, vc, preferred_element_type=F32)
    hpg = n_heads // N_KV
    imps = []
    for g in range(N_KV):
        ph = p[g * hpg * rows:(g * hpg + 1) * rows]
        for h in range(1, hpg):
            ph = ph + p[(g * hpg + h) * rows:(g * hpg + h + 1) * rows]
        imps.append(ph[:, :nc // 2] + ph[:, nc // 2:])
    return o, imps


_ROW_SPLIT = 2


def _attn_prompt_body(q_ref, g_ref, kc_ref, vc_ref, ksa_ref, vsa_ref, kwb_ref, vwa_ref, o_ref,
                      qa_ref, m_ref, acc_ref, s_ref, *, tq, tk, n_heads):
    t0 = pl.program_id(1) * tq
    rows = n_heads * tq
    hpg = n_heads // N_KV
    hr = rows // _ROW_SPLIT
    for h in range(n_heads):
        qa_ref[h * tq:(h + 1) * tq, 0:LANES] = q_ref[:, h * LANES:(h + 1) * LANES]
    qc = qa_ref[:, 0:LANES]
    qpos = t0 + lax.broadcasted_iota(jnp.int32, (tq, 1), 0)

    nc = kc_ref.shape[0]
    col = lax.broadcasted_iota(jnp.int32, (tq, nc), 1)
    cblk = jnp.where(col < nc // 2, 2 * col, 2 * (col - nc // 2) + 1)
    valid = (cblk + 1) * L_CMP - 1 <= qpos
    o_cmp, imps = _cmp_attend(qc, kc_ref[...], vc_ref[...], valid, n_heads, tq)

    blk = lax.broadcasted_iota(jnp.int32, (tq, LANES), 1)
    cur = qpos // L_SEL
    forced = (blk == 0) | (blk == cur) | (blk == cur - 1)
    started = blk * L_SEL <= qpos
    v = [jnp.where(forced, jnp.inf, jnp.where(started, imp, NEG)) for imp in imps]
    bias = _select_bias(jnp.concatenate(v, axis=0), N_SEL - _N_FORCED)
    for h in range(n_heads):
        g = h // hpg
        qa_ref[h * tq:(h + 1) * tq, LANES:2 * LANES] = bias[g * tq:(g + 1) * tq].astype(BF16)

    span = WINDOW + tq
    w0 = pl.multiple_of(jnp.maximum(t0 - WINDOW, 0), tq)
    kposw = w0 + lax.broadcasted_iota(jnp.int32, (tq, span), 1)
    band = (kposw <= qpos) & (kposw > qpos - WINDOW)
    o_wins = []
    for r in range(_ROW_SPLIT):
        s = lax.dot_general(qa_ref[r * hr:(r + 1) * hr, 0:LANES], kwb_ref[pl.ds(w0, span), :], _NT,
                            preferred_element_type=F32)
        s = _mask_heads(s, band, tq, NEG)
        p = jnp.exp(s - _rep(_rowmax(s), span // LANES)).astype(BF16)
        accw = jnp.dot(p, vwa_ref[pl.ds(w0, span), :], preferred_element_type=F32)
        o_wins.append(accw[:, 0:LANES] / accw[:, LANES:2 * LANES])
    o_win = jnp.concatenate(o_wins, axis=0)

    m_ref[...] = jnp.full(m_ref.shape, -jnp.inf, F32)
    acc_ref[...] = jnp.zeros(acc_ref.shape, F32)

    def scores(kt, slot):
        ks0 = pl.multiple_of(kt * tk, tk)
        s_ref[slot] = lax.dot_general(qa_ref[...], ksa_ref[pl.ds(ks0, tk), :], _NT, preferred_element_type=F32)

    def accumulate(kt, slot, causal):
        ks0 = pl.multiple_of(kt * tk, tk)
        s = s_ref[slot]
        if causal:
            kpos = ks0 + lax.broadcasted_iota(jnp.int32, (tq, tk), 1)
            s = _mask_heads(s, kpos <= qpos, tq, NEG)
        m_prev = m_ref[...]
        m_new = jnp.maximum(m_prev, jnp.max(s, axis=-1, keepdims=True))
        alpha = jnp.exp(m_prev - m_new)
        p = jnp.exp(s - _rep(m_new, tk // LANES)).astype(BF16)
        pv = jnp.dot(p, vsa_ref[pl.ds(ks0, tk), :], preferred_element_type=F32)
        acc_ref[...] = acc_ref[...] * _rep(alpha, 2) + pv
        m_ref[...] = m_new

    n_full = t0 // tk
    scores(0, 0)

    def tile_pair(jj, carry):
        j = 2 * jj
        scores(j + 1, 1)
        accumulate(j, 0, False)
        scores(j + 2, 0)
        accumulate(j + 1, 1, False)
        return carry

    lax.fori_loop(0, n_full // 2, tile_pair, 0)

    @pl.when(n_full % 2 == 1)
    def _():
        scores(n_full, 1)
        accumulate(n_full - 1, 0, False)
        accumulate(n_full, 1, True)

    @pl.when(n_full % 2 == 0)
    def _():
        accumulate(n_full, 0, True)
    acc = acc_ref[...]
    o_sel = acc[:, 0:LANES] / acc[:, LANES:2 * LANES]

    o_ref[...] = _gate_mix(g_ref[...], o_cmp, o_sel, o_win, tq).astype(BF16)


def _attn_prompt(q, gates, kc, vc, ksa, vsa, kwb, vwa, *, tq, tk):
    batch, seq, qw = q.shape
    n_heads = qw // LANES
    per_b = lambda a: pl.BlockSpec((None,) + a.shape[1:], lambda b, i: (b,) + (0,) * (a.ndim - 1))
    row = lambda c: pl.BlockSpec((None, tq, c), lambda b, i: (b, i, 0))
    return pl.pallas_call(
        functools.partial(_attn_prompt_body, tq=tq, tk=tk, n_heads=n_heads),
        out_shape=jax.ShapeDtypeStruct((batch, seq, n_heads * HEAD_DIM), BF16),
        grid=(batch, seq // tq),
        in_specs=[row(qw), row(LANES), per_b(kc), per_b(vc), per_b(ksa), per_b(vsa), per_b(kwb), per_b(vwa)],
        out_specs=row(n_heads * HEAD_DIM),
        scratch_shapes=[pltpu.VMEM((n_heads * tq, 2 * LANES), BF16),
                        pltpu.VMEM((n_heads * tq, LANES), F32),
                        pltpu.VMEM((n_heads * tq, 2 * LANES), F32),
                        pltpu.VMEM((2, n_heads * tq, tk), F32)],
        compiler_params=_params(("arbitrary", "arbitrary")),
        name="attn_prompt",
    )(q, gates, kc, vc, ksa, vsa, kwb, vwa)


_TP = SUBLANES
_KEY_CHUNKS = 4
_DMA_UNROLL = 4


def _attn_sample_body(pt_ref, q_ref, g_ref, new_ref, swk_ref, swv_ref, wck_ref, wcv_ref, perm_ref,
                      cck_hbm, ccv_hbm, csk_hbm, csv_hbm, o_ref,
                      bufs, sems, cs_ref, ka_ref, va_ref, kn_ref, vn_ref, kwn_ref, vwn_ref, vw_ref, qa_ref, s_ref,
                      *, n_heads, n_tok, n_pages, page):
    i = pl.program_id(0)
    n_seq = pl.num_programs(0)
    caches = (cck_hbm, ccv_hbm, csk_hbm, csv_hbm)
    hpg = n_heads // N_KV
    rows = n_heads * _TP
    past = n_pages * page
    wb = swk_ref.shape[1]

    slot = i % 2

    def page_copy(a, seq, slot_, p):
        return pltpu.make_async_copy(caches[a].at[pt_ref[seq, p]], bufs.at[a, slot_, p], sems.at[a, slot_])

    def start(seq, slot_):
        def body(p, c):
            for a in range(4):
                page_copy(a, seq, slot_, p).start(priority=a % 2)
            return c
        lax.fori_loop(0, n_pages, body, 0, unroll=_DMA_UNROLL)

    def wait(a):
        def body(p, c):
            page_copy(a, 0, slot, p).wait()
            return c
        lax.fori_loop(0, n_pages, body, 0, unroll=True)

    def cast_pages(a, dst):
        for p in range(n_pages):
            dst[0:KV_W, p * page:(p + 1) * page] = bufs[a, slot, p].astype(BF16)

    @pl.when(i == 0)
    def _():
        start(0, 0)
        blk = lax.broadcasted_iota(jnp.int32, (LANES, past), 0)
        kpos = lax.broadcasted_iota(jnp.int32, (LANES, past), 1)
        ka_ref[KV_W:, :] = jnp.where(kpos // L_SEL == blk, 1.0, 0.0).astype(BF16)
        vw_ref[KV_W:, :] = jnp.ones((LANES, wb), BF16)
        kn_ref[...] = jnp.zeros(kn_ref.shape, BF16)
        vn_ref[...] = jnp.zeros(vn_ref.shape, BF16)
        kwn_ref[...] = jnp.zeros(kwn_ref.shape, BF16)
        vwn_ref[...] = jnp.zeros(vwn_ref.shape, BF16)
        qa_ref[...] = jnp.zeros(qa_ref.shape, BF16)

    @pl.when(i + 1 < n_seq)
    def _():
        start(i + 1, 1 - slot)

    for h in range(n_heads):
        qa_ref[h * _TP:h * _TP + n_tok, 0:LANES] = q_ref[:, h * LANES:(h + 1) * LANES]
    qc = qa_ref[:, 0:LANES]
    tok = lax.broadcasted_iota(jnp.int32, (_TP, 1), 0)

    wait(0)
    wait(1)
    cast_pages(0, cs_ref.at[0])
    cast_pages(1, cs_ref.at[1])

    def summaries(a, w_ref):
        half = past // 2
        nat = jnp.concatenate(
            [jnp.dot(cs_ref[a, :, j * half:(j + 1) * half], w_ref[...], preferred_element_type=F32)
             for j in range(2)], axis=1).astype(BF16)
        return jnp.dot(nat, perm_ref[...], preferred_element_type=F32).astype(BF16)

    kct = summaries(0, wck_ref)
    vct = summaries(1, wcv_ref)
    o_cmp, imps = _cmp_attend(qc, kct, vct, None, n_heads, _TP, transposed=True)

    n_blk = past // L_SEL
    blk = lax.broadcasted_iota(jnp.int32, (_TP, LANES), 1)
    forced = (blk == 0) | (blk == n_blk - 1)
    v = jnp.concatenate([jnp.where(forced, jnp.inf, imp) for imp in imps]
                        + [jnp.full((LANES - N_KV * _TP, LANES), NEG, F32)], axis=0)
    bias = _select_bias(v, N_SEL - _N_FORCED)
    for h in range(n_heads):
        g = h // hpg
        qa_ref[h * _TP:(h + 1) * _TP, LANES:] = bias[g * _TP:(g + 1) * _TP].astype(BF16)

    wait(2)
    wait(3)
    cast_pages(2, ka_ref)
    cast_pages(3, va_ref)

    new = new_ref[...]
    kn_ref[0:n_tok, :] = new[:, 2 * KV_W:3 * KV_W].astype(BF16)
    vn_ref[0:n_tok, 0:LANES] = new[:, 3 * KV_W:4 * KV_W].astype(BF16)
    vn_ref[0:n_tok, LANES:] = jnp.ones((n_tok, LANES), BF16)
    jn = lax.broadcasted_iota(jnp.int32, (_TP, LANES), 1)
    new_ok = (jn <= tok) & (jn < n_tok)
    chunk = past // _KEY_CHUNKS
    for c in range(_KEY_CHUNKS):
        ks = slice(c * chunk, (c + 1) * chunk)
        s_ref[:, ks] = jnp.dot(qa_ref[...], ka_ref[:, ks], preferred_element_type=F32)
    s_new = lax.dot_general(qc, kn_ref[...], _NT, preferred_element_type=F32)
    s_new = _mask_heads(s_new, new_ok, _TP, NEG)
    m = _rowmax(s_new)
    for c in range(_KEY_CHUNKS):
        m = jnp.maximum(m, _rowmax(s_ref[:, c * chunk:(c + 1) * chunk]))
    acc = jnp.dot(jnp.exp(s_new - m).astype(BF16), vn_ref[...], preferred_element_type=F32)
    num, den = acc[:, 0:LANES], acc[:, LANES:]
    for c in range(_KEY_CHUNKS):
        ks = slice(c * chunk, (c + 1) * chunk)
        e = jnp.exp(s_ref[:, ks] - _rep(m, chunk // LANES))
        den = den + _rowsum(e)
        num = num + lax.dot_general(e.astype(BF16), va_ref[:, ks], _NT, preferred_element_type=F32)
    o_sel = num / den

    kwn_ref[0:n_tok, :] = new[:, 4 * KV_W:5 * KV_W].astype(BF16)
    vwn_ref[0:n_tok, 0:LANES] = new[:, 5 * KV_W:6 * KV_W].astype(BF16)
    vwn_ref[0:n_tok, LANES:] = jnp.ones((n_tok, LANES), BF16)
    vw_ref[0:KV_W, :] = swv_ref[...].astype(BF16)
    s_wp = jnp.dot(qc, swk_ref[...].astype(BF16), preferred_element_type=F32)
    s_wn = lax.dot_general(qc, kwn_ref[...], _NT, preferred_element_type=F32)
    c = lax.broadcasted_iota(jnp.int32, (_TP, wb), 1)
    s_wp = _mask_heads(s_wp, c - wb > tok - WINDOW, _TP, NEG)
    s_wn = _mask_heads(s_wn, new_ok, _TP, NEG)
    m = jnp.maximum(_rowmax(s_wp), _rowmax(s_wn))
    accw = (lax.dot_general(jnp.exp(s_wp - _rep(m, wb // LANES)).astype(BF16), vw_ref[...], _NT,
                            preferred_element_type=F32)
            + jnp.dot(jnp.exp(s_wn - m).astype(BF16), vwn_ref[...], preferred_element_type=F32))
    o_win = accw[:, 0:LANES] / accw[:, LANES:]

    gates = jnp.concatenate([g_ref[...], jnp.zeros((_TP - n_tok, LANES), F32)], axis=0)
    o_ref[...] = _gate_mix(gates, o_cmp, o_sel, o_win, _TP)[0:n_tok].astype(BF16)


def _attn_sample(page_table, q, gates, new, swk, swv, wck, wcv, perm, cck, ccv, csk, csv):
    n_seq, n_tok, qw = q.shape
    n_heads = qw // LANES
    n_pages = page_table.shape[1]
    page = cck.shape[2]
    past = n_pages * page
    wb = swk.shape[2]
    per_seq = lambda a: pl.BlockSpec((None,) + a.shape[1:], lambda i, pt: (i,) + (0,) * (a.ndim - 1))
    hbm = pl.BlockSpec(memory_space=pl.ANY)
    grid_spec = pltpu.PrefetchScalarGridSpec(
        num_scalar_prefetch=1,
        grid=(n_seq,),
        in_specs=[per_seq(q), per_seq(gates), per_seq(new), per_seq(swk), per_seq(swv),
                  _resident(wck.shape), _resident(wcv.shape), _resident(perm.shape), hbm, hbm, hbm, hbm],
        out_specs=pl.BlockSpec((None, n_tok, n_heads * HEAD_DIM), lambda i, pt: (i, 0, 0)),
        scratch_shapes=[pltpu.VMEM((4, 2, n_pages, KV_W, page), F32),
                        pltpu.SemaphoreType.DMA((4, 2)),
                        pltpu.VMEM((2, KV_W, past), BF16),
                        pltpu.VMEM((2 * LANES, past), BF16),
                        pltpu.VMEM((KV_W, past), BF16),
                        pltpu.VMEM((LANES, KV_W), BF16),
                        pltpu.VMEM((LANES, 2 * LANES), BF16),
                        pltpu.VMEM((LANES, KV_W), BF16),
                        pltpu.VMEM((LANES, 2 * LANES), BF16),
                        pltpu.VMEM((2 * LANES, wb), BF16),
                        pltpu.VMEM((n_heads * _TP, 2 * LANES), BF16),
                        pltpu.VMEM((n_heads * _TP, past), F32)])
    return pl.pallas_call(
        functools.partial(_attn_sample_body, n_heads=n_heads, n_tok=n_tok, n_pages=n_pages, page=page),
        out_shape=jax.ShapeDtypeStruct((n_seq, n_tok, n_heads * HEAD_DIM), BF16),
        grid_spec=grid_spec,
        compiler_params=_params(("arbitrary",)),
        name="attn_sample",
    )(page_table, q, gates, new, swk, swv, wck, wcv, perm, cck, ccv, csk, csv)


def _cmp_placement(w, n_keys):
    n_blk = n_keys // L_CMP
    hit = (jnp.arange(n_keys) // L_CMP)[:, None] == jnp.arange(n_blk)[None, :]
    return jnp.where(hit, jnp.tile(w, n_blk)[:, None], 0.0).astype(BF16)


def _even_odd_perm(nc):
    src = jnp.arange(nc)
    dst = src // 2 + (src % 2) * (nc // 2)
    return (dst[:, None] == jnp.arange(nc)[None, :]).astype(BF16)


def _out_ffn_body(on_ref, yp_ref, x_ref, g2_ref, sh_ref, sc_ref, gt_ref, gains_ref, wout_ref, wi_ref, wo_ref,
                  o_ref, act_ref, *, d_ff):
    half = on_ref.shape[1]
    y = (jnp.dot(on_ref[...], wout_ref[0:half, :], preferred_element_type=F32)
         + jnp.dot(yp_ref[...], wout_ref[half:, :], preferred_element_type=F32))
    x = x_ref[...] + g2_ref[...] * _rms(y, gains_ref[0:1, :])
    h = (_rms(x, gains_ref[1:2, :]) * (1 + sc_ref[...]) + sh_ref[...]).astype(BF16)
    y = _swiglu(h, wi_ref, wo_ref, act_ref, d_ff)
    o_ref[...] = x + 0.5 * gt_ref[...] * _rms(y, gains_ref[2:3, :])


def _out_ffn(o_nsa, y_pool, x, mod, ks, gains, w_out, wi, wo, *, tm, per_row, tiles_per_seq):
    r, d = x.shape
    d_ff = wo.shape[0]
    row = lambda c: pl.BlockSpec((tm, c), lambda i: (i, 0))
    return pl.pallas_call(
        functools.partial(_out_ffn_body, d_ff=d_ff),
        out_shape=jax.ShapeDtypeStruct((r, d), F32),
        grid=(r // tm,),
        in_specs=[row(o_nsa.shape[1]), row(y_pool.shape[1]), row(d)]
                 + [_mod_spec(k, per_row, tm, d, tiles_per_seq) for k in ks]
                 + [_resident((3, d)), _resident(w_out.shape), _resident(wi.shape), _resident(wo.shape)],
        out_specs=row(d),
        scratch_shapes=[pltpu.VMEM((tm, d_ff), BF16)],
        compiler_params=_params(("arbitrary",)),
        name="out_ffn",
    )(o_nsa, y_pool, x, *([mod] * len(ks)), gains, w_out, wi, wo)


def _even_odd(a):
    return jnp.concatenate([a[:, 0::2], a[:, 1::2]], axis=1).astype(BF16)


def kernel(x_prompt, x_sample, cache_cmp_k, cache_cmp_v, cache_sel_k, cache_sel_v, state_win_k, state_win_v,
           state_pool, page_table, c_prompt, c_sample, w_ada, b_ada, norm_gains, ffn1_wi, ffn1_wo, ffn2_wi,
           ffn2_wo, w_in, w_out, cmp_w, pool_w, pool_scale):
    depth = w_ada.shape[0]
    batch, seq, d = x_prompt.shape
    n_seq, n_tok, _ = x_sample.shape
    n_pages = page_table.shape[1]
    page = cache_cmp_k.shape[2]
    past = n_pages * page
    tm, tq, tk = 512, 128, 512
    tm_ffn = 1024
    assert depth == 1 and seq % tm_ffn == 0 and seq // L_SEL == LANES and past // L_SEL == LANES
    assert n_tok <= _TP and past % L_SEL == 0 and past >= max(POOL_WINDOWS) and seq >= WINDOW + tq
    assert state_pool.shape[2] == max(POOL_WINDOWS) - 1 and cache_cmp_k.shape[3:] == (N_KV, HEAD_DIM)
    l = 0
    n_rows_s = n_seq * n_tok

    mod = _modulation(jnp.concatenate([c_sample, c_prompt], axis=0), w_ada[l], b_ada[l])
    mod_p = mod[:, n_seq:n_seq + batch, None, :]
    mod_s = jnp.tile(mod[:, :n_seq], (1, n_tok, 1))
    gains = norm_gains[l]
    wi1, wo1 = ffn1_wi[l].astype(BF16), ffn1_wo[l].astype(BF16)
    wi2, wo2 = ffn2_wi[l].astype(BF16), ffn2_wo[l].astype(BF16)
    w_all = _prep_w_in(w_in[l])
    w_o = w_out[l].astype(BF16)
    pw = pool_w[l].astype(BF16)
    ps = pool_scale[l].reshape(1, -1)
    cmp_rows_p = jnp.tile(cmp_w[l][:, :, None], (1, tm // L_CMP, KV_W))

    xp = x_prompt.reshape(batch * seq, d)
    xs = jnp.swapaxes(x_sample, 0, 1).reshape(n_rows_s, d)

    xp = _ffn(xp, mod_p, (0, 1, 2), gains[0:2], wi1, wo1, tm=tm_ffn, per_row=False,
              tiles_per_seq=seq // tm_ffn)
    xs = _ffn(xs, mod_s, (0, 1, 2), gains[0:2], wi1, wo1, tm=n_rows_s, per_row=True, tiles_per_seq=1)

    (kcr, vcr, ks, vs, kw, vw, kc, vc, qp, gp, ksa, vsa, kwb, vwa, yp_p, u_tail) = _inproj_prompt(
        xp, mod_p, (3, 4), gains[2:3], w_all, cmp_rows_p, pw, ps, batch=batch, seq=seq, tm=tm)
    on_p = _attn_prompt(qp, gp, _even_odd(kc), _even_odd(vc), ksa, vsa, kwb, vwa, tq=tq, tk=tk)

    hist = jnp.swapaxes(state_pool[l], 0, 1)
    kv_s, q_s, g_s, yp_s, u_s = _inproj_sample(
        xs, mod_s, (3, 4), gains[2:3], w_all, hist, pw, ps, n_seq=n_seq, n_tok=n_tok, past_len=past)
    by_seq = lambda a: jnp.swapaxes(a.reshape(n_tok, n_seq, a.shape[-1]), 0, 1)
    new_s = by_seq(kv_s)
    cols = lambda a: jnp.transpose(a, (0, 2, 3, 1)).reshape(a.shape[0], KV_W, a.shape[1])
    on_s = _attn_sample(page_table, by_seq(q_s), by_seq(g_s), new_s, cols(state_win_k[l]), cols(state_win_v[l]),
                        _cmp_placement(cmp_w[l, 0], past // 2), _cmp_placement(cmp_w[l, 1], past // 2),
                        _even_odd_perm(past // L_CMP),
                        cols(cache_cmp_k[l]), cols(cache_cmp_v[l]), cols(cache_sel_k[l]), cols(cache_sel_v[l]))
    on_s = jnp.swapaxes(on_s, 0, 1).reshape(n_rows_s, -1)

    xp = _out_ffn(on_p.reshape(batch * seq, -1), yp_p.reshape(batch * seq, -1), xp,
                  mod_p, (5, 6, 7, 8), gains[3:6], w_o, wi2, wo2, tm=tm_ffn, per_row=False,
                  tiles_per_seq=seq // tm_ffn)
    xs = _out_ffn(on_s, yp_s, xs, mod_s, (5, 6, 7, 8), gains[3:6], w_o, wi2, wo2,
                  tm=n_rows_s, per_row=True, tiles_per_seq=1)

    heads = lambda a: a.reshape(a.shape[:-1] + (N_KV, HEAD_DIM))[None]
    keep = min(WINDOW, seq)
    n_state = state_pool.shape[2]
    y_prompt = xp.reshape(batch, seq, d)
    y_sample = jnp.swapaxes(xs.reshape(n_tok, n_seq, d), 0, 1)
    p_out = (heads(kcr), heads(vcr), heads(ks), heads(vs), heads(kw[:, seq - keep:]), heads(vw[:, seq - keep:]),
             u_tail[:, _HALO - n_state:][None])
    new4 = [heads(new_s[..., j * KV_W:(j + 1) * KV_W]) for j in range(6)]
    keep_s = min(WINDOW, state_win_k.shape[2] + n_tok)
    win_k = jnp.concatenate([state_win_k, new4[4]], axis=2)
    win_v = jnp.concatenate([state_win_v, new4[5]], axis=2)
    pool_s = jnp.concatenate([state_pool[l], by_seq(u_s)], axis=1)
    s_out = (new4[0], new4[1], new4[2], new4[3], win_k[:, :, win_k.shape[2] - keep_s:],
             win_v[:, :, win_v.shape[2] - keep_s:], pool_s[:, pool_s.shape[1] - n_state:][None])
    return (y_prompt, y_sample) + p_out + s_out
```

```python
import functools

import jax
import jax.numpy as jnp
from jax import lax
from jax.experimental import pallas as pl
from jax.experimental.pallas import tpu as pltpu

HEAD_DIM = 64
N_KV = 2
L_CMP = 32
L_SEL = 64
N_SEL = 16
WINDOW = 512
POOL_WINDOWS = (2, 4, 8, 16)
N_MOD = 9
RMS_EPS = 1e-6
NEG = -1e30

LANES = 128
SUBLANES = 8
VMEM_LIMIT_BYTES = 56 * 1024 * 1024

F32 = jnp.float32
BF16 = jnp.bfloat16

KV_W = N_KV * HEAD_DIM
_NT = (((1,), (1,)), ((), ()))


def _rms(x, g):
    return x * lax.rsqrt(jnp.mean(x * x, axis=-1, keepdims=True) + RMS_EPS) * g


def _silu(a):
    return a * jax.nn.sigmoid(a)


def _params(sem):
    return pltpu.CompilerParams(dimension_semantics=sem, vmem_limit_bytes=VMEM_LIMIT_BYTES)


def _resident(shape):
    return pl.BlockSpec(shape, lambda *_: (0,) * len(shape), pipeline_mode=pl.Buffered(1))


def _mod_body(c_ref, w_ref, b_ref, o_ref):
    a = _silu(c_ref[...]).astype(BF16)
    o_ref[...] = jnp.dot(a, w_ref[...].astype(BF16), preferred_element_type=F32) + b_ref[...]


def _modulation(c, w_ada, b_ada):
    n, d = c.shape
    n_pad = -(-n // SUBLANES) * SUBLANES
    c = jnp.pad(c, ((0, n_pad - n), (0, 0)))
    return pl.pallas_call(
        _mod_body,
        out_shape=jax.ShapeDtypeStruct((N_MOD, n_pad, d), F32),
        grid=(N_MOD,),
        in_specs=[pl.BlockSpec((n_pad, d), lambda j: (0, 0)),
                  pl.BlockSpec((d, d), lambda j: (0, j)),
                  pl.BlockSpec((1, d), lambda j: (0, j))],
        out_specs=pl.BlockSpec((None, n_pad, d), lambda j: (j, 0, 0)),
        compiler_params=_params(("arbitrary",)),
        name="adaln_modulation",
    )(c, w_ada, b_ada.reshape(1, -1))


def _ffn_chunks(d_ff):
    chunks, o = [], 0
    while o < d_ff:
        w = min(512, d_ff - o)
        chunks.append((o, w))
        o += w
    return chunks


def _swiglu(h, wi_ref, wo_ref, act_ref, d_ff):
    for o, w in _ffn_chunks(d_ff):
        a = jnp.dot(h, wi_ref[:, o:o + w], preferred_element_type=F32)
        b = jnp.dot(h, wi_ref[:, d_ff + o:d_ff + o + w], preferred_element_type=F32)
        act_ref[:, o:o + w] = (_silu(a) * b).astype(BF16)
    return jnp.dot(act_ref[...], wo_ref[...], preferred_element_type=F32)


def _ffn_body(x_ref, sh_ref, sc_ref, gt_ref, gains_ref, wi_ref, wo_ref, o_ref, act_ref, *, d_ff):
    x = x_ref[...]
    h = (_rms(x, gains_ref[0:1, :]) * (1 + sc_ref[...]) + sh_ref[...]).astype(BF16)
    y = _swiglu(h, wi_ref, wo_ref, act_ref, d_ff)
    o_ref[...] = x + 0.5 * gt_ref[...] * _rms(y, gains_ref[1:2, :])


def _mod_spec(k, per_row, tm, d, tiles_per_seq):
    if per_row:
        return pl.BlockSpec((None, tm, d), lambda i: (k, i, 0))
    return pl.BlockSpec((None, None, 1, d), lambda i: (k, i // tiles_per_seq, 0, 0))


def _ffn(x, mod, ks, gains, wi, wo, *, tm, per_row, tiles_per_seq):
    r, d = x.shape
    d_ff = wo.shape[0]
    return pl.pallas_call(
        functools.partial(_ffn_body, d_ff=d_ff),
        out_shape=jax.ShapeDtypeStruct((r, d), F32),
        grid=(r // tm,),
        in_specs=[pl.BlockSpec((tm, d), lambda i: (i, 0))]
                 + [_mod_spec(k, per_row, tm, d, tiles_per_seq) for k in ks]
                 + [_resident((2, d)), _resident(wi.shape), _resident(wo.shape)],
        out_specs=pl.BlockSpec((tm, d), lambda i: (i, 0)),
        scratch_shapes=[pltpu.VMEM((tm, d_ff), BF16)],
        compiler_params=_params(("arbitrary",)),
        name="ffn",
    )(x, *([mod] * len(ks)), gains, wi, wo)


_QW = 8 * LANES
_O_KC, _O_VC, _O_KS, _O_VS, _O_KW, _O_VW = (_QW + i * KV_W for i in range(6))
_O_G = _QW + 6 * KV_W
_O_U = _O_G + LANES
_POOL_W = len(POOL_WINDOWS) * LANES
_IN_COLS = _O_U + _POOL_W
_HALO = 16


def _prep_w_in(w_in):
    d = w_in.shape[0]
    n_heads = (_QW // LANES)
    hpg = n_heads // N_KV
    q = w_in[:, :n_heads * HEAD_DIM].reshape(d, n_heads, 1, HEAD_DIM)
    own_group = (jnp.arange(n_heads) // hpg)[:, None] == jnp.arange(N_KV)[None, :]
    slot = jnp.where(own_group[None, :, :, None], q, 0.0)
    o = n_heads * HEAD_DIM
    kv = w_in[:, o:o + 6 * KV_W]
    o += 6 * KV_W
    n_g = n_heads * 3
    g = jnp.pad(w_in[:, o:o + n_g], ((0, 0), (0, LANES - n_g)))
    u = w_in[:, o + n_g:]
    return jnp.concatenate([slot.reshape(d, _QW), kv, g, u], axis=1).astype(BF16)


def _project(x_ref, sh_ref, sc_ref, gain_ref, w_ref):
    h = (_rms(x_ref[...], gain_ref[...]) * (1 + sc_ref[...]) + sh_ref[...]).astype(BF16)
    return jnp.dot(h, w_ref[...], preferred_element_type=F32)


def _pool_out(d, pw_ref, ps_ref):
    parts = [jnp.dot(d[:, g * LANES:(g + 1) * LANES].astype(BF16), pw_ref[g], preferred_element_type=F32)
             for g in range(len(POOL_WINDOWS))]
    return (jnp.concatenate(parts, axis=1) * ps_ref[...]).astype(BF16)


def _inproj_prompt_body(x_ref, sh_ref, sc_ref, gain_ref, w_ref, cw_ref, pw_ref, ps_ref,
                        kcr_ref, vcr_ref, ks_ref, vs_ref, kw_ref, vw_ref, kc_ref, vc_ref,
                        q_ref, g_ref, ksa_ref, vsa_ref, kwb_ref, vwa_ref, yp_ref, ul_ref,
                        ext_ref, *, tm):
    i = pl.program_id(1)
    z = _project(x_ref, sh_ref, sc_ref, gain_ref, w_ref)
    kcr, vcr = z[:, _O_KC:_O_KC + KV_W], z[:, _O_VC:_O_VC + KV_W]
    ks, vs = z[:, _O_KS:_O_KS + KV_W], z[:, _O_VS:_O_VS + KV_W]
    kw, vw = z[:, _O_KW:_O_KW + KV_W], z[:, _O_VW:_O_VW + KV_W]
    kcr_ref[...], vcr_ref[...] = kcr, vcr
    ks_ref[...], vs_ref[...] = ks, vs
    kw_ref[...], vw_ref[...] = kw, vw
    kc_ref[...] = (kcr * cw_ref[0]).reshape(tm // L_CMP, L_CMP, KV_W).sum(axis=1)
    vc_ref[...] = (vcr * cw_ref[1]).reshape(tm // L_CMP, L_CMP, KV_W).sum(axis=1)
    q_ref[...] = (z[:, :_QW] * (HEAD_DIM ** -0.5)).astype(BF16)
    g_ref[...] = jax.nn.sigmoid(z[:, _O_G:_O_G + LANES])
    pos = i * tm + lax.broadcasted_iota(jnp.int32, (tm, LANES), 0)
    blk = lax.broadcasted_iota(jnp.int32, (tm, LANES), 1)
    onehot = jnp.where(pos // L_SEL == blk, 1.0, 0.0).astype(BF16)
    ones = jnp.ones((tm, LANES), BF16)
    ksa_ref[...] = jnp.concatenate([ks.astype(BF16), onehot], axis=1)
    vsa_ref[...] = jnp.concatenate([vs.astype(BF16), ones], axis=1)
    kwb_ref[...] = kw.astype(BF16)
    vwa_ref[...] = jnp.concatenate([vw.astype(BF16), ones], axis=1)
    u = z[:, _O_U:_O_U + _POOL_W]

    @pl.when(i == 0)
    def _():
        ext_ref[0:_HALO, :] = jnp.zeros((_HALO, _POOL_W), F32)

    ext_ref[_HALO:, :] = u
    tpos = i * tm + lax.broadcasted_iota(jnp.int32, (tm, LANES), 0)
    diffs = []
    for g, w in enumerate(POOL_WINDOWS):
        s = ext_ref[:, g * LANES:(g + 1) * LANES]
        k = 1
        while k < w:
            s = s + pltpu.roll(s, k, 0)
            k *= 2
        cnt = jnp.minimum(w, tpos + 1).astype(F32)
        diffs.append(s[_HALO:] / cnt - u[:, g * LANES:(g + 1) * LANES])
    yp_ref[...] = _pool_out(jnp.concatenate(diffs, axis=1), pw_ref, ps_ref)
    tail = u[tm - _HALO:, :]
    ext_ref[0:_HALO, :] = tail
    ul_ref[...] = tail


def _inproj_prompt(x, mod, ks, gain, w_all, cmp_rows, pool_w, pool_scale, *, batch, seq, tm):
    d = x.shape[1]
    tps = seq // tm
    row = lambda c: pl.BlockSpec((None, tm, c), lambda b, i: (b, i, 0))
    mods = [pl.BlockSpec((None, None, 1, d), lambda b, i, k=k: (k, b, 0, 0)) for k in ks]
    f32o = lambda c: jax.ShapeDtypeStruct((batch, seq, c), F32)
    bf16o = lambda c: jax.ShapeDtypeStruct((batch, seq, c), BF16)
    out_shape = ([f32o(KV_W)] * 6
                 + [jax.ShapeDtypeStruct((batch, seq // L_CMP, KV_W), F32)] * 2
                 + [bf16o(_QW), f32o(LANES), bf16o(2 * LANES), bf16o(2 * LANES), bf16o(KV_W), bf16o(2 * LANES),
                    bf16o(_POOL_W), jax.ShapeDtypeStruct((batch, _HALO, _POOL_W), F32)])
    out_specs = ([row(KV_W)] * 6
                 + [pl.BlockSpec((None, tm // L_CMP, KV_W), lambda b, i: (b, i, 0))] * 2
                 + [row(_QW), row(LANES), row(2 * LANES), row(2 * LANES), row(KV_W), row(2 * LANES),
                    row(_POOL_W), pl.BlockSpec((None, _HALO, _POOL_W), lambda b, i: (b, 0, 0))])
    return pl.pallas_call(
        functools.partial(_inproj_prompt_body, tm=tm),
        out_shape=out_shape,
        grid=(batch, tps),
        in_specs=[pl.BlockSpec((tm, d), lambda b, i: (b * tps + i, 0))] + mods
                 + [_resident((1, d)), _resident(w_all.shape), _resident(cmp_rows.shape),
                    _resident(pool_w.shape), _resident(pool_scale.shape)],
        out_specs=out_specs,
        scratch_shapes=[pltpu.VMEM((_HALO + tm, _POOL_W), F32)],
        compiler_params=_params(("arbitrary", "arbitrary")),
        name="inproj_prompt",
    )(x, *([mod] * len(ks)), gain, w_all, cmp_rows, pool_w, pool_scale)


def _inproj_sample_body(x_ref, sh_ref, sc_ref, gain_ref, w_ref, hist_ref, pw_ref, ps_ref,
                        kv_ref, q_ref, g_ref, yp_ref, u_ref, *, n_seq, n_tok, past_len):
    z = _project(x_ref, sh_ref, sc_ref, gain_ref, w_ref)
    kv_ref[...] = z[:, _O_KC:_O_KC + 6 * KV_W]
    q_ref[...] = (z[:, :_QW] * (HEAD_DIM ** -0.5)).astype(BF16)
    g_ref[...] = jax.nn.sigmoid(z[:, _O_G:_O_G + LANES])
    u = z[:, _O_U:_O_U + _POOL_W]
    u_ref[...] = u
    n_hist = hist_ref.shape[0]
    slabs = [hist_ref[j] for j in range(n_hist)] + [u[t * n_seq:(t + 1) * n_seq] for t in range(n_tok)]
    rows = []
    for t in range(n_tok):
        diffs = []
        for g, w in enumerate(POOL_WINDOWS):
            lanes = slice(g * LANES, (g + 1) * LANES)
            s = slabs[n_hist + t][:, lanes]
            for j in range(1, w):
                s = s + slabs[n_hist + t - j][:, lanes]
            diffs.append(s / float(min(w, past_len + t + 1)) - slabs[n_hist + t][:, lanes])
        rows.append(jnp.concatenate(diffs, axis=1))
    yp_ref[...] = _pool_out(jnp.concatenate(rows, axis=0), pw_ref, ps_ref)


def _inproj_sample(x, mod, ks, gain, w_all, hist, pool_w, pool_scale, *, n_seq, n_tok, past_len):
    r, d = x.shape
    full = lambda a: pl.BlockSpec(a.shape, lambda i: (0,) * a.ndim)
    out_shape = [jax.ShapeDtypeStruct((r, 6 * KV_W), F32), jax.ShapeDtypeStruct((r, _QW), BF16),
                 jax.ShapeDtypeStruct((r, LANES), F32), jax.ShapeDtypeStruct((r, _POOL_W), BF16),
                 jax.ShapeDtypeStruct((r, _POOL_W), F32)]
    rest = (gain, w_all, hist, pool_w, pool_scale)
    return pl.pallas_call(
        functools.partial(_inproj_sample_body, n_seq=n_seq, n_tok=n_tok, past_len=past_len),
        out_shape=out_shape,
        grid=(1,),
        in_specs=[full(x)] + [_mod_spec(k, True, r, d, 1) for k in ks] + [full(a) for a in rest],
        out_specs=[pl.BlockSpec(s.shape, lambda i: (0, 0)) for s in out_shape],
        compiler_params=_params(("arbitrary",)),
        name="inproj_sample",
    )(x, *([mod] * len(ks)), *rest)


_N_FORCED = 3


def _select_bias(v, n_pick):
    vt = v.T
    blk = lax.broadcasted_iota(jnp.int32, vt.shape, 0).astype(F32)
    always = vt == jnp.inf
    bias = jnp.where(always, 0.0, NEG)
    vt = jnp.where(always, -jnp.inf, vt)
    for _ in range(n_pick):
        m = jnp.max(vt, axis=0, keepdims=True)
        first = jnp.min(jnp.where(vt == m, blk, float(LANES)), axis=0, keepdims=True)
        hit = blk == first
        bias = jnp.where(hit, 0.0, bias)
        vt = jnp.where(hit, -jnp.inf, vt)
    return bias.T


def _rep(m, n):
    return jnp.concatenate([m] * n, axis=1) if n > 1 else m


def _rowmax(s):
    return jnp.broadcast_to(jnp.max(s, axis=-1, keepdims=True), (s.shape[0], LANES))


def _rowsum(s):
    return jnp.broadcast_to(jnp.sum(s, axis=-1, keepdims=True), (s.shape[0], LANES))


def _mask_heads(s, mask, rows, fill):
    n, c = s.shape[0] // rows, s.shape[1]
    return jnp.where(mask[None], s.reshape(n, rows, c), fill).reshape(n * rows, c)


def _gate_mix(gates, o_cmp, o_sel, o_win, rows):
    n_heads = o_cmp.shape[0] // rows
    hpg = n_heads // N_KV
    parts = []
    for h in range(n_heads):
        g = h // hpg
        lanes = slice(g * HEAD_DIM, (g + 1) * HEAD_DIM)
        rs = slice(h * rows, (h + 1) * rows)
        parts.append(gates[:, 3 * h:3 * h + 1] * o_cmp[rs, lanes]
                     + gates[:, 3 * h + 1:3 * h + 2] * o_sel[rs, lanes]
                     + gates[:, 3 * h + 2:3 * h + 3] * o_win[rs, lanes])
    return jnp.concatenate(parts, axis=1)


def _cmp_attend(qc, kc, vc, valid, n_heads, rows, transposed=False):
    nc = kc.shape[1] if transposed else kc.shape[0]
    if transposed:
        s = jnp.dot(qc, kc, preferred_element_type=F32)
    else:
        s = lax.dot_general(qc, kc, _NT, preferred_element_type=F32)
    if valid is not None:
        s = _mask_heads(s, valid, rows, NEG)
    e = jnp.exp(s - _rep(_rowmax(s), nc // LANES))
    p = e / _rep(_rowsum(e), nc // LANES)
    if valid is not None:
        p = _mask_heads(p, valid, rows, 0.0)
    pb = p.astype(BF16)
    if transposed:
        o = lax.dot_general(pb, vc, _NT, preferred_element_type=F32)
    else:
        o = jnp.dot(pb, vc, preferred_element_type=F32)
    hpg = n_heads // N_KV
    imps = []
    for g in range(N_KV):
        ph = p[g * hpg * rows:(g * hpg + 1) * rows]
        for h in range(1, hpg):
            ph = ph + p[(g * hpg + h) * rows:(g * hpg + h + 1) * rows]
        imps.append(ph[:, :nc // 2] + ph[:, nc // 2:])
    return o, imps


_ROW_SPLIT = 2


def _attn_prompt_body(q_ref, g_ref, kc_ref, vc_ref, ksa_ref, vsa_ref, kwb_ref, vwa_ref, o_ref,
                      qa_ref, m_ref, acc_ref, s_ref, *, tq, tk, n_heads):
    t0 = pl.program_id(1) * tq
    rows = n_heads * tq
    hpg = n_heads // N_KV
    hr = rows // _ROW_SPLIT
    for h in range(n_heads):
        qa_ref[h * tq:(h + 1) * tq, 0:LANES] = q_ref[:, h * LANES:(h + 1) * LANES]
    qc = qa_ref[:, 0:LANES]
    qpos = t0 + lax.broadcasted_iota(jnp.int32, (tq, 1), 0)

    nc = kc_ref.shape[0]
    col = lax.broadcasted_iota(jnp.int32, (tq, nc), 1)
    cblk = jnp.where(col < nc // 2, 2 * col, 2 * (col - nc // 2) + 1)
    valid = (cblk + 1) * L_CMP - 1 <= qpos
    o_cmp, imps = _cmp_attend(qc, kc_ref[...], vc_ref[...], valid, n_heads, tq)

    blk = lax.broadcasted_iota(jnp.int32, (tq, LANES), 1)
    cur = qpos // L_SEL
    forced = (blk == 0) | (blk == cur) | (blk == cur - 1)
    started = blk * L_SEL <= qpos
    v = [jnp.where(forced, jnp.inf, jnp.where(started, imp, NEG)) for imp in imps]
    bias = _select_bias(jnp.concatenate(v, axis=0), N_SEL - _N_FORCED)
    for h in range(n_heads):
        g = h // hpg
        qa_ref[h * tq:(h + 1) * tq, LANES:2 * LANES] = bias[g * tq:(g + 1) * tq].astype(BF16)

    span = WINDOW + tq
    w0 = pl.multiple_of(jnp.maximum(t0 - WINDOW, 0), tq)
    kposw = w0 + lax.broadcasted_iota(jnp.int32, (tq, span), 1)
    band = (kposw <= qpos) & (kposw > qpos - WINDOW)
    o_wins = []
    for r in range(_ROW_SPLIT):
        s = lax.dot_general(qa_ref[r * hr:(r + 1) * hr, 0:LANES], kwb_ref[pl.ds(w0, span), :], _NT,
                            preferred_element_type=F32)
        s = _mask_heads(s, band, tq, NEG)
        p = jnp.exp(s - _rep(_rowmax(s), span // LANES)).astype(BF16)
        accw = jnp.dot(p, vwa_ref[pl.ds(w0, span), :], preferred_element_type=F32)
        o_wins.append(accw[:, 0:LANES] / accw[:, LANES:2 * LANES])
    o_win = jnp.concatenate(o_wins, axis=0)

    m_ref[...] = jnp.full(m_ref.shape, -jnp.inf, F32)
    acc_ref[...] = jnp.zeros(acc_ref.shape, F32)

    def scores(kt, slot):
        ks0 = pl.multiple_of(kt * tk, tk)
        s_ref[slot] = lax.dot_general(qa_ref[...], ksa_ref[pl.ds(ks0, tk), :], _NT, preferred_element_type=F32)

    def accumulate(kt, slot, causal):
        ks0 = pl.multiple_of(kt * tk, tk)
        s = s_ref[slot]
        if causal:
            kpos = ks0 + lax.broadcasted_iota(jnp.int32, (tq, tk), 1)
            s = _mask_heads(s, kpos <= qpos, tq, NEG)
        m_prev = m_ref[...]
        m_new = jnp.maximum(m_prev, jnp.max(s, axis=-1, keepdims=True))
        alpha = jnp.exp(m_prev - m_new)
        p = jnp.exp(s - _rep(m_new, tk // LANES)).astype(BF16)
        pv = jnp.dot(p, vsa_ref[pl.ds(ks0, tk), :], preferred_element_type=F32)
        acc_ref[...] = acc_ref[...] * _rep(alpha, 2) + pv
        m_ref[...] = m_new

    n_full = t0 // tk
    scores(0, 0)

    def tile_pair(jj, carry):
        j = 2 * jj
        scores(j + 1, 1)
        accumulate(j, 0, False)
        scores(j + 2, 0)
        accumulate(j + 1, 1, False)
        return carry

    lax.fori_loop(0, n_full // 2, tile_pair, 0)

    @pl.when(n_full % 2 == 1)
    def _():
        scores(n_full, 1)
        accumulate(n_full - 1, 0, False)
        accumulate(n_full, 1, True)

    @pl.when(n_full % 2 == 0)
    def _():
        accumulate(n_full, 0, True)
    acc = acc_ref[...]
    o_sel = acc[:, 0:LANES] / acc[:, LANES:2 * LANES]

    o_ref[...] = _gate_mix(g_ref[...], o_cmp, o_sel, o_win, tq).astype(BF16)


def _attn_prompt(q, gates, kc, vc, ksa, vsa, kwb, vwa, *, tq, tk):
    batch, seq, qw = q.shape
    n_heads = qw // LANES
    per_b = lambda a: pl.BlockSpec((None,) + a.shape[1:], lambda b, i: (b,) + (0,) * (a.ndim - 1))
    row = lambda c: pl.BlockSpec((None, tq, c), lambda b, i: (b, i, 0))
    return pl.pallas_call(
        functools.partial(_attn_prompt_body, tq=tq, tk=tk, n_heads=n_heads),
        out_shape=jax.ShapeDtypeStruct((batch, seq, n_heads * HEAD_DIM), BF16),
        grid=(batch, seq // tq),
        in_specs=[row(qw), row(LANES), per_b(kc), per_b(vc), per_b(ksa), per_b(vsa), per_b(kwb), per_b(vwa)],
        out_specs=row(n_heads * HEAD_DIM),
        scratch_shapes=[pltpu.VMEM((n_heads * tq, 2 * LANES), BF16),
                        pltpu.VMEM((n_heads * tq, LANES), F32),
                        pltpu.VMEM((n_heads * tq, 2 * LANES), F32),
                        pltpu.VMEM((2, n_heads * tq, tk), F32)],
        compiler_params=_params(("arbitrary", "arbitrary")),
        name="attn_prompt",
    )(q, gates, kc, vc, ksa, vsa, kwb, vwa)


_TP = SUBLANES
_KEY_CHUNKS = 4
_DMA_UNROLL = 4


def _attn_sample_body(pt_ref, q_ref, g_ref, new_ref, swk_ref, swv_ref, wck_ref, wcv_ref, perm_ref,
                      cck_hbm, ccv_hbm, csk_hbm, csv_hbm, o_ref,
                      bufs, sems, cs_ref, ka_ref, va_ref, kn_ref, vn_ref, kwn_ref, vwn_ref, vw_ref, qa_ref, s_ref,
                      *, n_heads, n_tok, n_pages, page):
    i = pl.program_id(0)
    n_seq = pl.num_programs(0)
    caches = (cck_hbm, ccv_hbm, csk_hbm, csv_hbm)
    hpg = n_heads // N_KV
    rows = n_heads * _TP
    past = n_pages * page
    wb = swk_ref.shape[1]

    slot = i % 2

    def page_copy(a, seq, slot_, p):
        return pltpu.make_async_copy(caches[a].at[pt_ref[seq, p]], bufs.at[a, slot_, p], sems.at[a, slot_])

    def start(seq, slot_):
        def body(p, c):
            for a in range(4):
                page_copy(a, seq, slot_, p).start(priority=a % 2)
            return c
        lax.fori_loop(0, n_pages, body, 0, unroll=_DMA_UNROLL)

    def wait(a, slot_):
        def body(p, c):
            page_copy(a, 0, slot_, p).wait()
            return c
        lax.fori_loop(0, n_pages, body, 0, unroll=True)

    nxt = jnp.minimum(i + 1, n_seq - 1)

    def cast_pages(a, dst):
        for p in range(n_pages):
            dst[0:KV_W, p * page:(p + 1) * page] = bufs[a, slot, p].astype(BF16)
            page_copy(a, nxt, 1 - slot, p).start(priority=a % 2)

    @pl.when(i == 0)
    def _():
        start(0, 0)
        blk = lax.broadcasted_iota(jnp.int32, (LANES, past), 0)
        kpos = lax.broadcasted_iota(jnp.int32, (LANES, past), 1)
        ka_ref[KV_W:, :] = jnp.where(kpos // L_SEL == blk, 1.0, 0.0).astype(BF16)
        vw_ref[KV_W:, :] = jnp.ones((LANES, wb), BF16)
        kn_ref[...] = jnp.zeros(kn_ref.shape, BF16)
        vn_ref[...] = jnp.zeros(vn_ref.shape, BF16)
        kwn_ref[...] = jnp.zeros(kwn_ref.shape, BF16)
        vwn_ref[...] = jnp.zeros(vwn_ref.shape, BF16)
        qa_ref[...] = jnp.zeros(qa_ref.shape, BF16)

    for h in range(n_heads):
        qa_ref[h * _TP:h * _TP + n_tok, 0:LANES] = q_ref[:, h * LANES:(h + 1) * LANES]
    qc = qa_ref[:, 0:LANES]
    tok = lax.broadcasted_iota(jnp.int32, (_TP, 1), 0)

    wait(0, slot)
    wait(1, slot)
    cast_pages(0, cs_ref.at[0])
    cast_pages(1, cs_ref.at[1])

    def summaries(a, w_ref):
        half = past // 2
        nat = jnp.concatenate(
            [jnp.dot(cs_ref[a, :, j * half:(j + 1) * half], w_ref[...], preferred_element_type=F32)
             for j in range(2)], axis=1).astype(BF16)
        return jnp.dot(nat, perm_ref[...], preferred_element_type=F32).astype(BF16)

    kct = summaries(0, wck_ref)
    vct = summaries(1, wcv_ref)
    o_cmp, imps = _cmp_attend(qc, kct, vct, None, n_heads, _TP, transposed=True)

    n_blk = past // L_SEL
    blk = lax.broadcasted_iota(jnp.int32, (_TP, LANES), 1)
    forced = (blk == 0) | (blk == n_blk - 1)
    v = jnp.concatenate([jnp.where(forced, jnp.inf, imp) for imp in imps]
                        + [jnp.full((LANES - N_KV * _TP, LANES), NEG, F32)], axis=0)
    bias = _select_bias(v, N_SEL - _N_FORCED)
    for h in range(n_heads):
        g = h // hpg
        qa_ref[h * _TP:(h + 1) * _TP, LANES:] = bias[g * _TP:(g + 1) * _TP].astype(BF16)

    wait(2, slot)
    wait(3, slot)
    cast_pages(2, ka_ref)
    cast_pages(3, va_ref)

    new = new_ref[...]
    kn_ref[0:n_tok, :] = new[:, 2 * KV_W:3 * KV_W].astype(BF16)
    vn_ref[0:n_tok, 0:LANES] = new[:, 3 * KV_W:4 * KV_W].astype(BF16)
    vn_ref[0:n_tok, LANES:] = jnp.ones((n_tok, LANES), BF16)
    jn = lax.broadcasted_iota(jnp.int32, (_TP, LANES), 1)
    new_ok = (jn <= tok) & (jn < n_tok)
    chunk = past // _KEY_CHUNKS
    for c in range(_KEY_CHUNKS):
        ks = slice(c * chunk, (c + 1) * chunk)
        s_ref[:, ks] = jnp.dot(qa_ref[...], ka_ref[:, ks], preferred_element_type=F32)
    s_new = lax.dot_general(qc, kn_ref[...], _NT, preferred_element_type=F32)
    s_new = _mask_heads(s_new, new_ok, _TP, NEG)
    m = _rowmax(s_new)
    for c in range(_KEY_CHUNKS):
        m = jnp.maximum(m, _rowmax(s_ref[:, c * chunk:(c + 1) * chunk]))
    acc = jnp.dot(jnp.exp(s_new - m).astype(BF16), vn_ref[...], preferred_element_type=F32)
    num, den = acc[:, 0:LANES], acc[:, LANES:]
    for c in range(_KEY_CHUNKS):
        ks = slice(c * chunk, (c + 1) * chunk)
        e = jnp.exp(s_ref[:, ks] - _rep(m, chunk // LANES))
        den = den + _rowsum(e)
        num = num + lax.dot_general(e.astype(BF16), va_ref[:, ks], _NT, preferred_element_type=F32)
    o_sel = num / den

    kwn_ref[0:n_tok, :] = new[:, 4 * KV_W:5 * KV_W].astype(BF16)
    vwn_ref[0:n_tok, 0:LANES] = new[:, 5 * KV_W:6 * KV_W].astype(BF16)
    vwn_ref[0:n_tok, LANES:] = jnp.ones((n_tok, LANES), BF16)
    vw_ref[0:KV_W, :] = swv_ref[...].astype(BF16)
    s_wp = jnp.dot(qc, swk_ref[...].astype(BF16), preferred_element_type=F32)
    s_wn = lax.dot_general(qc, kwn_ref[...], _NT, preferred_element_type=F32)
    c = lax.broadcasted_iota(jnp.int32, (_TP, wb), 1)
    s_wp = _mask_heads(s_wp, c - wb > tok - WINDOW, _TP, NEG)
    s_wn = _mask_heads(s_wn, new_ok, _TP, NEG)
    m = jnp.maximum(_rowmax(s_wp), _rowmax(s_wn))
    accw = (lax.dot_general(jnp.exp(s_wp - _rep(m, wb // LANES)).astype(BF16), vw_ref[...], _NT,
                            preferred_element_type=F32)
            + jnp.dot(jnp.exp(s_wn - m).astype(BF16), vwn_ref[...], preferred_element_type=F32))
    o_win = accw[:, 0:LANES] / accw[:, LANES:]

    gates = jnp.concatenate([g_ref[...], jnp.zeros((_TP - n_tok, LANES), F32)], axis=0)
    o_ref[...] = _gate_mix(gates, o_cmp, o_sel, o_win, _TP)[0:n_tok].astype(BF16)

    @pl.when(i == n_seq - 1)
    def _():
        for a in range(4):
            wait(a, 1 - slot)


def _attn_sample(page_table, q, gates, new, swk, swv, wck, wcv, perm, cck, ccv, csk, csv):
    n_seq, n_tok, qw = q.shape
    n_heads = qw // LANES
    n_pages = page_table.shape[1]
    page = cck.shape[2]
    past = n_pages * page
    wb = swk.shape[2]
    per_seq = lambda a: pl.BlockSpec((None,) + a.shape[1:], lambda i, pt: (i,) + (0,) * (a.ndim - 1))
    hbm = pl.BlockSpec(memory_space=pl.ANY)
    grid_spec = pltpu.PrefetchScalarGridSpec(
        num_scalar_prefetch=1,
        grid=(n_seq,),
        in_specs=[per_seq(q), per_seq(gates), per_seq(new), per_seq(swk), per_seq(swv),
                  _resident(wck.shape), _resident(wcv.shape), _resident(perm.shape), hbm, hbm, hbm, hbm],
        out_specs=pl.BlockSpec((None, n_tok, n_heads * HEAD_DIM), lambda i, pt: (i, 0, 0)),
        scratch_shapes=[pltpu.VMEM((4, 2, n_pages, KV_W, page), F32),
                        pltpu.SemaphoreType.DMA((4, 2)),
                        pltpu.VMEM((2, KV_W, past), BF16),
                        pltpu.VMEM((2 * LANES, past), BF16),
                        pltpu.VMEM((KV_W, past), BF16),
                        pltpu.VMEM((LANES, KV_W), BF16),
                        pltpu.VMEM((LANES, 2 * LANES), BF16),
                        pltpu.VMEM((LANES, KV_W), BF16),
                        pltpu.VMEM((LANES, 2 * LANES), BF16),
                        pltpu.VMEM((2 * LANES, wb), BF16),
                        pltpu.VMEM((n_heads * _TP, 2 * LANES), BF16),
                        pltpu.VMEM((n_heads * _TP, past), F32)])
    return pl.pallas_call(
        functools.partial(_attn_sample_body, n_heads=n_heads, n_tok=n_tok, n_pages=n_pages, page=page),
        out_shape=jax.ShapeDtypeStruct((n_seq, n_tok, n_heads * HEAD_DIM), BF16),
        grid_spec=grid_spec,
        compiler_params=_params(("arbitrary",)),
        name="attn_sample",
    )(page_table, q, gates, new, swk, swv, wck, wcv, perm, cck, ccv, csk, csv)


def _cmp_placement(w, n_keys):
    n_blk = n_keys // L_CMP
    hit = (jnp.arange(n_keys) // L_CMP)[:, None] == jnp.arange(n_blk)[None, :]
    return jnp.where(hit, jnp.tile(w, n_blk)[:, None], 0.0).astype(BF16)


def _even_odd_perm(nc):
    src = jnp.arange(nc)
    dst = src // 2 + (src % 2) * (nc // 2)
    return (dst[:, None] == jnp.arange(nc)[None, :]).astype(BF16)


def _out_ffn_body(on_ref, yp_ref, x_ref, g2_ref, sh_ref, sc_ref, gt_ref, gains_ref, wout_ref, wi_ref, wo_ref,
                  o_ref, act_ref, *, d_ff):
    half = on_ref.shape[1]
    y = (jnp.dot(on_ref[...], wout_ref[0:half, :], preferred_element_type=F32)
         + jnp.dot(yp_ref[...], wout_ref[half:, :], preferred_element_type=F32))
    x = x_ref[...] + g2_ref[...] * _rms(y, gains_ref[0:1, :])
    h = (_rms(x, gains_ref[1:2, :]) * (1 + sc_ref[...]) + sh_ref[...]).astype(BF16)
    y = _swiglu(h, wi_ref, wo_ref, act_ref, d_ff)
    o_ref[...] = x + 0.5 * gt_ref[...] * _rms(y, gains_ref[2:3, :])


def _out_ffn(o_nsa, y_pool, x, mod, ks, gains, w_out, wi, wo, *, tm, per_row, tiles_per_seq):
    r, d = x.shape
    d_ff = wo.shape[0]
    row = lambda c: pl.BlockSpec((tm, c), lambda i: (i, 0))
    return pl.pallas_call(
        functools.partial(_out_ffn_body, d_ff=d_ff),
        out_shape=jax.ShapeDtypeStruct((r, d), F32),
        grid=(r // tm,),
        in_specs=[row(o_nsa.shape[1]), row(y_pool.shape[1]), row(d)]
                 + [_mod_spec(k, per_row, tm, d, tiles_per_seq) for k in ks]
                 + [_resident((3, d)), _resident(w_out.shape), _resident(wi.shape), _resident(wo.shape)],
        out_specs=row(d),
        scratch_shapes=[pltpu.VMEM((tm, d_ff), BF16)],
        compiler_params=_params(("arbitrary",)),
        name="out_ffn",
    )(o_nsa, y_pool, x, *([mod] * len(ks)), gains, w_out, wi, wo)


def _even_odd(a):
    return jnp.concatenate([a[:, 0::2], a[:, 1::2]], axis=1).astype(BF16)


def kernel(x_prompt, x_sample, cache_cmp_k, cache_cmp_v, cache_sel_k, cache_sel_v, state_win_k, state_win_v,
           state_pool, page_table, c_prompt, c_sample, w_ada, b_ada, norm_gains, ffn1_wi, ffn1_wo, ffn2_wi,
           ffn2_wo, w_in, w_out, cmp_w, pool_w, pool_scale):
    depth = w_ada.shape[0]
    batch, seq, d = x_prompt.shape
    n_seq, n_tok, _ = x_sample.shape
    n_pages = page_table.shape[1]
    page = cache_cmp_k.shape[2]
    past = n_pages * page
    tm, tq, tk = 512, 128, 512
    tm_ffn = 1024
    assert depth == 1 and seq % tm_ffn == 0 and seq // L_SEL == LANES and past // L_SEL == LANES
    assert n_tok <= _TP and past % L_SEL == 0 and past >= max(POOL_WINDOWS) and seq >= WINDOW + tq
    assert state_pool.shape[2] == max(POOL_WINDOWS) - 1 and cache_cmp_k.shape[3:] == (N_KV, HEAD_DIM)
    l = 0
    n_rows_s = n_seq * n_tok

    mod = _modulation(jnp.concatenate([c_sample, c_prompt], axis=0), w_ada[l], b_ada[l])
    mod_p = mod[:, n_seq:n_seq + batch, None, :]
    mod_s = jnp.tile(mod[:, :n_seq], (1, n_tok, 1))
    gains = norm_gains[l]
    wi1, wo1 = ffn1_wi[l].astype(BF16), ffn1_wo[l].astype(BF16)
    wi2, wo2 = ffn2_wi[l].astype(BF16), ffn2_wo[l].astype(BF16)
    w_all = _prep_w_in(w_in[l])
    w_o = w_out[l].astype(BF16)
    pw = pool_w[l].astype(BF16)
    ps = pool_scale[l].reshape(1, -1)
    cmp_rows_p = jnp.tile(cmp_w[l][:, :, None], (1, tm // L_CMP, KV_W))

    xp = x_prompt.reshape(batch * seq, d)
    xs = jnp.swapaxes(x_sample, 0, 1).reshape(n_rows_s, d)

    xp = _ffn(xp, mod_p, (0, 1, 2), gains[0:2], wi1, wo1, tm=tm_ffn, per_row=False,
              tiles_per_seq=seq // tm_ffn)
    xs = _ffn(xs, mod_s, (0, 1, 2), gains[0:2], wi1, wo1, tm=n_rows_s, per_row=True, tiles_per_seq=1)

    (kcr, vcr, ks, vs, kw, vw, kc, vc, qp, gp, ksa, vsa, kwb, vwa, yp_p, u_tail) = _inproj_prompt(
        xp, mod_p, (3, 4), gains[2:3], w_all, cmp_rows_p, pw, ps, batch=batch, seq=seq, tm=tm)
    on_p = _attn_prompt(qp, gp, _even_odd(kc), _even_odd(vc), ksa, vsa, kwb, vwa, tq=tq, tk=tk)

    hist = jnp.swapaxes(state_pool[l], 0, 1)
    kv_s, q_s, g_s, yp_s, u_s = _inproj_sample(
        xs, mod_s, (3, 4), gains[2:3], w_all, hist, pw, ps, n_seq=n_seq, n_tok=n_tok, past_len=past)
    by_seq = lambda a: jnp.swapaxes(a.reshape(n_tok, n_seq, a.shape[-1]), 0, 1)
    new_s = by_seq(kv_s)
    cols = lambda a: jnp.transpose(a, (0, 2, 3, 1)).reshape(a.shape[0], KV_W, a.shape[1])
    on_s = _attn_sample(page_table, by_seq(q_s), by_seq(g_s), new_s, cols(state_win_k[l]), cols(state_win_v[l]),
                        _cmp_placement(cmp_w[l, 0], past // 2), _cmp_placement(cmp_w[l, 1], past // 2),
                        _even_odd_perm(past // L_CMP),
                        cols(cache_cmp_k[l]), cols(cache_cmp_v[l]), cols(cache_sel_k[l]), cols(cache_sel_v[l]))
    on_s = jnp.swapaxes(on_s, 0, 1).reshape(n_rows_s, -1)

    xp = _out_ffn(on_p.reshape(batch * seq, -1), yp_p.reshape(batch * seq, -1), xp,
                  mod_p, (5, 6, 7, 8), gains[3:6], w_o, wi2, wo2, tm=tm_ffn, per_row=False,
                  tiles_per_seq=seq // tm_ffn)
    xs = _out_ffn(on_s, yp_s, xs, mod_s, (5, 6, 7, 8), gains[3:6], w_o, wi2, wo2,
                  tm=n_rows_s, per_row=True, tiles_per_seq=1)

    heads = lambda a: a.reshape(a.shape[:-1] + (N_KV, HEAD_DIM))[None]
    keep = min(WINDOW, seq)
    n_state = state_pool.shape[2]
    y_prompt = xp.reshape(batch, seq, d)
    y_sample = jnp.swapaxes(xs.reshape(n_tok, n_seq, d), 0, 1)
    p_out = (heads(kcr), heads(vcr), heads(ks), heads(vs), heads(kw[:, seq - keep:]), heads(vw[:, seq - keep:]),
             u_tail[:, _HALO - n_state:][None])
    new4 = [heads(new_s[..., j * KV_W:(j + 1) * KV_W]) for j in range(6)]
    keep_s = min(WINDOW, state_win_k.shape[2] + n_tok)
    win_k = jnp.concatenate([state_win_k, new4[4]], axis=2)
    win_v = jnp.concatenate([state_win_v, new4[5]], axis=2)
    pool_s = jnp.concatenate([state_pool[l], by_seq(u_s)], axis=1)
    s_out = (new4[0], new4[1], new4[2], new4[3], win_k[:, :, win_k.shape[2] - keep_s:],
             win_v[:, :, win_v.shape[2] - keep_s:], pool_s[:, pool_s.shape[1] - n_state:][None])
    return (y_prompt, y_sample) + p_out + s_out
```
